```python
import math
import jax, jax.numpy as jnp
from jax import lax
import numpy as np

D_MODEL = 2048
BATCH = 2
SEQ = 4096
DEPTH = 4
DEC_BATCH = 128
DEC_SEQ = 1
PAST_LEN = 8192
PAGE_SIZE = 128

N_MIXERS = 4
HEAD_DIM = 128
N_HEADS = D_MODEL // HEAD_DIM
D_ATTN = N_HEADS * HEAD_DIM
ATTN_SCALE = HEAD_DIM ** -0.5
D_FF = -(-8 * D_MODEL // (3 * 256)) * 256
RMS_EPS = 1e-6
REL_BUCKETS = 32
REL_MAX_DIST = 128
NEG_INF = -1e30
FORCE_SCORE = 1e30
GATHER_Q_BLOCK = 16
DENSE_Q_BLOCK = 128
MOBA_BLOCK = 256
MOBA_TOPK = 3
MOBA_KV_HEADS = 2
DIL_PATTERNS = ((128, 1), (512, 4), (2048, 16))
DIL_KV_HEADS = 4
CMP_BLOCK = 32
CMP_STRIDE = 16
CMP_HIDDEN = 256
SEL_BLOCK = 64
SEL_TOPK = 16
WIN = 512
WIN_BLOCK = 128
NSA_KV_ROWS = 6
MLA_Q_LORA = 512
MLA_KV_LORA = 512
MLA_NOPE = 128
MLA_ROPE = 64
MLA_V = 128
MLA_SCALE = (MLA_NOPE + MLA_ROPE) ** -0.5
ROPE_THETA = 10000.0

kernel_name = 'hybrid_moba_dilated_nsa_mla_step'

F32 = jnp.float32


def _rmsnorm(x, g):
    xf = x.astype(F32)
    y = xf * lax.rsqrt(jnp.mean(xf * xf, axis=-1, keepdims=True) + RMS_EPS)
    return (y * g.astype(F32)).astype(x.dtype)


def _rel_bucket(dist):
    n = jnp.maximum(dist, 0)
    max_exact = REL_BUCKETS // 2
    large = max_exact + (jnp.log(jnp.maximum(n, 1).astype(F32) / max_exact)
                         / math.log(REL_MAX_DIST / max_exact) * (REL_BUCKETS - max_exact)).astype(jnp.int32)
    large = jnp.minimum(large, REL_BUCKETS - 1)
    return jnp.where(n < max_exact, n, large)


def _softmax_lse(logits, mask, axis):
    l = jnp.where(mask, logits.astype(F32), NEG_INF)
    m = jnp.max(l, axis=axis, keepdims=True)
    e = jnp.where(mask, jnp.exp(l - m), 0.0)
    s = jnp.sum(e, axis=axis, keepdims=True)
    s = jnp.maximum(s, 1e-30)
    return e / s, jnp.squeeze(m + jnp.log(s), axis)


def _rope(x, pos):
    half = x.shape[-1] // 2
    inv = ROPE_THETA ** (-jnp.arange(half, dtype=F32) / half)
    ang = pos.astype(F32)[:, None] * inv[None, :]
    cos = jnp.cos(ang)[:, None, :]
    sin = jnp.sin(ang)[:, None, :]
    xf = x.astype(F32)
    x1, x2 = xf[..., :half], xf[..., half:]
    return jnp.concatenate([x1 * cos - x2 * sin, x1 * sin + x2 * cos], -1).astype(x.dtype)


def _gather_pages(pool, pt_row):
    g = pool[pt_row]
    return g.reshape((-1,) + pool.shape[2:])


def _map_query_blocks(fn, q_arrays, q_pos, block):
    T = q_pos.shape[0]
    if T <= block or T % block:
        return fn(*q_arrays, q_pos)
    nb = T // block
    xs = tuple(a.reshape((nb, block) + a.shape[1:]) for a in q_arrays) + (q_pos.reshape(nb, block),)
    out = lax.map(lambda a: fn(*a), xs)
    return out.reshape((T,) + out.shape[2:])


def _swiglu(h, w_in, w_out):
    gu = h @ w_in
    return (jax.nn.silu(gu[..., :D_FF]) * gu[..., D_FF:]) @ w_out


def _moba_seq(q, kv, q_pos, rel_bias):
    L = kv.shape[0]
    nb = -(-L // MOBA_BLOCK)
    kv = jnp.pad(kv, ((0, nb * MOBA_BLOCK - L), (0, 0), (0, 0), (0, 0)))
    kvb = kv.reshape(nb, MOBA_BLOCK, 2, MOBA_KV_HEADS, HEAD_DIM).transpose(2, 3, 0, 1, 4)
    kb, vb = kvb[0], kvb[1]
    kmean = jnp.mean(kb.astype(F32), axis=2)
    topk = min(MOBA_TOPK, nb)
    G = N_HEADS // MOBA_KV_HEADS
    tb = rel_bias.T.reshape(MOBA_KV_HEADS, G, REL_BUCKETS)
    kv_i = jnp.arange(MOBA_KV_HEADS)[None, :, None, None]
    g_i = jnp.arange(G)[None, None, :, None, None]

    def block_fn(qb, pos):
        Tb = qb.shape[0]
        qg = qb.reshape(Tb, MOBA_KV_HEADS, G, HEAD_DIM)
        own = pos // MOBA_BLOCK
        gate = jnp.einsum('tkgd,knd->tkgn', qg.astype(F32), kmean)
        past = jnp.arange(nb)[None, :] < own[:, None]
        gate = jnp.where(past[:, None, None, :], gate, NEG_INF)
        _, sel = lax.top_k(gate, topk)
        sel_ok = sel < own[:, None, None, None]
        own_b = jnp.broadcast_to(own[:, None, None, None], (Tb, MOBA_KV_HEADS, G, 1))
        blocks = jnp.concatenate([sel, own_b], -1)
        ok_blk = jnp.concatenate([sel_ok, jnp.ones_like(own_b, dtype=bool)], -1)
        kg = kb[kv_i, blocks]
        vg = vb[kv_i, blocks]
        kpos = blocks[..., None] * MOBA_BLOCK + jnp.arange(MOBA_BLOCK)
        dist = pos[:, None, None, None, None] - kpos
        logits = (jnp.einsum('tkgd,tkgnsd->tkgns', qg, kg).astype(F32) * ATTN_SCALE
                  + tb[kv_i[..., None], g_i, _rel_bucket(dist)])
        mask = ok_blk[..., None] & (dist >= 0)
        nk = (topk + 1) * MOBA_BLOCK
        p, _ = _softmax_lse(logits.reshape(Tb, MOBA_KV_HEADS, G, nk), mask.reshape(Tb, MOBA_KV_HEADS, G, nk), -1)
        out = jnp.einsum('tkgj,tkgjd->tkgd', p.astype(vg.dtype), vg.reshape(Tb, MOBA_KV_HEADS, G, nk, HEAD_DIM))
        return out.reshape(Tb, N_HEADS, HEAD_DIM)

    return _map_query_blocks(block_fn, (q,), q_pos, GATHER_Q_BLOCK)


def _mixer_moba(hp, hs, cache_a_kv, page_table, pos_p, pos_s, rel_bias, w_qkv, w_o):
    def proj(h):
        B, T, _ = h.shape
        qkv = h @ w_qkv
        q = qkv[..., :D_ATTN].reshape(B, T, N_HEADS, HEAD_DIM)
        kv = qkv[..., D_ATTN:].reshape(B, T, 2, MOBA_KV_HEADS, HEAD_DIM)
        return q, kv
    qp, kvp = proj(hp)
    qs, kvs = proj(hs)
    op = lax.map(lambda a: _moba_seq(a[0], a[1], pos_p, rel_bias), (qp, kvp))

    def sample_seq(a):
        q, kv_new, pt = a
        kv_all = jnp.concatenate([_gather_pages(cache_a_kv, pt), kv_new], axis=0)
        return _moba_seq(q, kv_all, pos_s, rel_bias)
    os_ = lax.map(sample_seq, (qs, kvs, page_table))
    yp = op.reshape(hp.shape[:2] + (D_ATTN,)) @ w_o
    ys = os_.reshape(hs.shape[:2] + (D_ATTN,)) @ w_o
    return yp, ys, kvp, kvs


def _dilated_prompt(q, kv, win, dil, rel_bias):
    B, S = q.shape[:2]
    R = win // dil
    Sp = -(-S // win) * win
    nbu = Sp // win
    G = N_HEADS // DIL_KV_HEADS
    q = jnp.pad(q, ((0, 0), (0, Sp - S), (0, 0), (0, 0)))
    kv = jnp.pad(kv, ((0, 0), (0, Sp - S), (0, 0), (0, 0), (0, 0)))
    qr = q.reshape(B, nbu, R, dil, DIL_KV_HEADS, G, HEAD_DIM)
    kvr = kv.reshape(B, nbu, R, dil, 2, DIL_KV_HEADS, HEAD_DIM)
    prev = jnp.pad(kvr, ((0, 0), (1, 0), (0, 0), (0, 0), (0, 0), (0, 0), (0, 0)))[:, :nbu]
    kk = jnp.concatenate([prev, kvr], axis=2)
    a = jnp.arange(R)[:, None]
    j = jnp.arange(2 * R)[None, :]
    steps = R + a - j
    band = (steps >= 0) & (steps <= R)
    kvalid = (jnp.arange(nbu)[:, None] > 0) | (jnp.arange(2 * R)[None, :] >= R)
    ok = band[None] & kvalid[:, None, :]
    bias = rel_bias[_rel_bucket(steps * dil)].transpose(2, 0, 1).reshape(DIL_KV_HEADS, G, R, 2 * R)
    logits = jnp.einsum('bnqrkgd,bnjrkd->bnrkgqj', qr, kk[:, :, :, :, 0]).astype(F32) * ATTN_SCALE + bias
    p, lse = _softmax_lse(logits, ok[None, :, None, None, None, :, :], -1)
    out = jnp.einsum('bnrkgqj,bnjrkd->bnqrkgd', p.astype(kk.dtype), kk[:, :, :, :, 1])
    out = out.reshape(B, Sp, N_HEADS, HEAD_DIM)[:, :S]
    lse = lse.transpose(0, 1, 5, 2, 3, 4).reshape(B, Sp, N_HEADS)[:, :S]
    return out, lse


def _dilated_sample(q, kv_new, buf, win, dil, rel_bias):
    DB, T = q.shape[:2]
    Lbuf = buf.shape[1]
    R = win // dil
    G = N_HEADS // DIL_KV_HEADS
    ext = jnp.concatenate([buf, kv_new], axis=1)
    m = jnp.arange(R + 1)
    idx = Lbuf + jnp.arange(T)[:, None] - m[None, :] * dil
    ok = idx >= 0
    g = ext[:, jnp.clip(idx, 0)]
    qg = q.reshape(DB, T, DIL_KV_HEADS, G, HEAD_DIM)
    bias = rel_bias[_rel_bucket(m * dil)].T.reshape(DIL_KV_HEADS, G, R + 1)
    logits = jnp.einsum('btkgd,btjkd->btkgj', qg, g[:, :, :, 0]).astype(F32) * ATTN_SCALE + bias
    p, lse = _softmax_lse(logits, ok[None, :, None, None, :], -1)
    out = jnp.einsum('btkgj,btjkd->btkgd', p.astype(g.dtype), g[:, :, :, 1])
    return out.reshape(DB, T, N_HEADS, HEAD_DIM), lse.reshape(DB, T, N_HEADS), ext[:, T:]


def _merge_groups(outs, lses):
    w = jax.nn.softmax(jnp.stack(lses, 0), axis=0)
    o = jnp.sum(w[..., None] * jnp.stack(outs, 0).astype(F32), axis=0)
    B, T = o.shape[:2]
    return o.reshape(B, T, D_ATTN).astype(outs[0].dtype)


def _mixer_dilated(hp, hs, state_b_kv1, state_b_kv2, state_b_kv3, rel_bias, w_qkv, w_o):
    bufs = (state_b_kv1, state_b_kv2, state_b_kv3)
    n_g = len(DIL_PATTERNS)
    width = D_ATTN + 2 * DIL_KV_HEADS * HEAD_DIM

    def proj(h):
        B, T, _ = h.shape
        qkv = (h @ w_qkv).reshape(B, T, n_g, width)
        q = qkv[..., :D_ATTN].reshape(B, T, n_g, N_HEADS, HEAD_DIM)
        kv = qkv[..., D_ATTN:].reshape(B, T, n_g, 2, DIL_KV_HEADS, HEAD_DIM)
        return q, kv
    qp, kvp = proj(hp)
    qs, kvs = proj(hs)
    S = hp.shape[1]
    outs_p, lses_p, outs_s, lses_s, new_p, new_s = [], [], [], [], [], []
    for gi, (win, dil) in enumerate(DIL_PATTERNS):
        o, l = _dilated_prompt(qp[:, :, gi], kvp[:, :, gi], win, dil, rel_bias)
        outs_p.append(o)
        lses_p.append(l)
        new_p.append(kvp[:, S - min(win, S):, gi])
        o, l, nbuf = _dilated_sample(qs[:, :, gi], kvs[:, :, gi], bufs[gi], win, dil, rel_bias)
        outs_s.append(o)
        lses_s.append(l)
        new_s.append(nbuf)
    yp = _merge_groups(outs_p, lses_p) @ w_o
    ys = _merge_groups(outs_s, lses_s) @ w_o
    return yp, ys, new_p, new_s


def _compress(kv2, w1, w2, pe):
    L = kv2.shape[0]
    n_cmp = (L - CMP_BLOCK) // CMP_STRIDE + 1
    idx = jnp.arange(n_cmp)[:, None] * CMP_STRIDE + jnp.arange(CMP_BLOCK)[None, :]
    blk = kv2[idx].transpose(2, 0, 1, 3) + pe[:, None]
    hid = jax.nn.gelu(jnp.einsum('cnx,cxh->cnh', blk.reshape(2, n_cmp, CMP_BLOCK * HEAD_DIM), w1))
    return jnp.einsum('cnh,chd->cnd', hid, w2)


def _nsa_seq(q, gates, kv4, q_pos, rel_bias, w_cmp1, w_cmp2, w_cmp_pe):
    L = kv4.shape[0]
    cmp = _compress(kv4[:, 0:2], w_cmp1, w_cmp2, w_cmp_pe)
    kc, vc = cmp[0], cmp[1]
    n_cmp = kc.shape[0]
    tok_start = jnp.arange(n_cmp) * CMP_STRIDE
    cmp_end = tok_start + CMP_BLOCK - 1
    n_sel = -(-L // SEL_BLOCK)
    sel_kv = jnp.pad(kv4[:, 2:4], ((0, n_sel * SEL_BLOCK - L), (0, 0), (0, 0))).reshape(n_sel, SEL_BLOCK, 2, HEAD_DIM)
    blk_start = jnp.arange(n_sel) * SEL_BLOCK
    overlap = ((tok_start[:, None] < blk_start[None, :] + SEL_BLOCK)
               & (tok_start[:, None] + CMP_BLOCK > blk_start[None, :])).astype(F32)
    n_top = min(SEL_TOPK, n_sel)

    def block_fn(qb, gb, pos):
        Tb = qb.shape[0]
        dc = pos[:, None] - cmp_end[None, :]
        lc = (jnp.einsum('thd,nd->thn', qb, kc).astype(F32) * ATTN_SCALE
              + rel_bias[_rel_bucket(dc)].transpose(0, 2, 1))
        pc, _ = _softmax_lse(lc, (dc >= 0)[:, None, :], -1)
        o_cmp = jnp.einsum('thn,nd->thd', pc.astype(vc.dtype), vc)
        imp = jnp.sum(pc, axis=1) @ overlap
        cur = pos // SEL_BLOCK
        j = jnp.arange(n_sel)[None, :]
        forced = (j == 0) | (j == cur[:, None]) | (j == cur[:, None] - 1)
        score = jnp.where(j > cur[:, None], NEG_INF, jnp.where(forced, FORCE_SCORE, imp))
        _, sidx = lax.top_k(score, n_top)
        g = sel_kv[sidx]
        kpos = sidx[..., None] * SEL_BLOCK + jnp.arange(SEL_BLOCK)
        ds = pos[:, None, None] - kpos
        ls = (jnp.einsum('thd,tnsd->thns', qb, g[..., 0, :]).astype(F32) * ATTN_SCALE
              + rel_bias[_rel_bucket(ds)].transpose(0, 3, 1, 2))
        nk = n_top * SEL_BLOCK
        ps, _ = _softmax_lse(ls.reshape(Tb, N_HEADS, nk), (ds >= 0).reshape(Tb, 1, nk), -1)
        o_sel = jnp.einsum('thj,tjd->thd', ps.astype(g.dtype), g[..., 1, :].reshape(Tb, nk, HEAD_DIM))
        return gb[..., 0:1] * o_cmp + gb[..., 1:2] * o_sel

    return _map_query_blocks(block_fn, (q, gates), q_pos, GATHER_Q_BLOCK)


def _window_prompt(q, kvw, rel_bias):
    B, S = q.shape[:2]
    P = WIN // WIN_BLOCK
    nb = S // WIN_BLOCK
    J = (P + 1) * WIN_BLOCK
    qr = q.reshape(B, nb, WIN_BLOCK, N_HEADS, HEAD_DIM)
    kvr = jnp.pad(kvw.reshape(B, nb, WIN_BLOCK, 2, HEAD_DIM), ((0, 0), (P, 0), (0, 0), (0, 0), (0, 0)))
    kk = jnp.concatenate([kvr[:, i:i + nb] for i in range(P + 1)], axis=2)
    dist = P * WIN_BLOCK + jnp.arange(WIN_BLOCK)[:, None] - jnp.arange(J)[None, :]
    kpos = jnp.arange(nb)[:, None] * WIN_BLOCK - P * WIN_BLOCK + jnp.arange(J)[None, :]
    ok = ((dist >= 0) & (dist <= WIN))[None] & (kpos >= 0)[:, None, :]
    bias = rel_bias[_rel_bucket(dist)].transpose(2, 0, 1)
    logits = jnp.einsum('bnqhd,bnjd->bnhqj', qr, kk[:, :, :, 0]).astype(F32) * ATTN_SCALE + bias
    p, _ = _softmax_lse(logits, ok[None, :, None], -1)
    out = jnp.einsum('bnhqj,bnjd->bnqhd', p.astype(kk.dtype), kk[:, :, :, 1])
    return out.reshape(B, S, N_HEADS, HEAD_DIM)


def _window_sample(q, kvw_new, buf, q_pos, rel_bias):
    T = q.shape[1]
    Lbuf = buf.shape[1]
    ext = jnp.concatenate([buf, kvw_new], axis=1)
    kpos = PAST_LEN - Lbuf + jnp.arange(Lbuf + T)
    dist = q_pos[:, None] - kpos[None, :]
    ok = (dist >= 0) & (dist <= WIN)
    bias = rel_bias[_rel_bucket(dist)].transpose(2, 0, 1)
    logits = jnp.einsum('bthd,bjd->bhtj', q, ext[:, :, 0]).astype(F32) * ATTN_SCALE + bias
    p, _ = _softmax_lse(logits, ok[None, None], -1)
    out = jnp.einsum('bhtj,bjd->bthd', p.astype(ext.dtype), ext[:, :, 1])
    return out, ext[:, T:]


def _mixer_nsa(hp, hs, cache_c_kv, state_c_win, page_table, pos_p, pos_s, rel_bias,
               w_qkv, w_gate, w_cmp1, w_cmp2, w_cmp_pe, w_o):
    def proj(h):
        B, T, _ = h.shape
        qkv = h @ w_qkv
        q = qkv[..., :D_ATTN].reshape(B, T, N_HEADS, HEAD_DIM)
        kv = qkv[..., D_ATTN:].reshape(B, T, NSA_KV_ROWS, HEAD_DIM)
        gates = jax.nn.sigmoid((h @ w_gate).astype(F32)).reshape(B, T, N_HEADS, 3).astype(h.dtype)
        return q, kv, gates
    qp, kvp, gp = proj(hp)
    qs, kvs, gs = proj(hs)
    op = lax.map(lambda a: _nsa_seq(a[0], a[1], a[2][:, :4], pos_p, rel_bias, w_cmp1, w_cmp2, w_cmp_pe),
                 (qp, gp, kvp))

    def sample_seq(a):
        q, g, kv_new, pt = a
        kv_all = jnp.concatenate([_gather_pages(cache_c_kv, pt), kv_new[:, :4]], axis=0)
        return _nsa_seq(q, g, kv_all, pos_s, rel_bias, w_cmp1, w_cmp2, w_cmp_pe)
    os_ = lax.map(sample_seq, (qs, gs, kvs, page_table))
    op = op + gp[..., 2:3] * _window_prompt(qp, kvp[:, :, 4:6], rel_bias)
    ow, win_s = _window_sample(qs, kvs[:, :, 4:6], state_c_win, pos_s, rel_bias)
    os_ = os_ + gs[..., 2:3] * ow
    S = hp.shape[1]
    yp = op.reshape(hp.shape[:2] + (D_ATTN,)) @ w_o
    ys = os_.reshape(hs.shape[:2] + (D_ATTN,)) @ w_o
    return yp, ys, kvp[:, :, :4], kvs[:, :, :4], kvp[:, S - min(WIN, S):, 4:6], win_s


def _mla_seq(q, lat, q_pos):
    kpos = jnp.arange(lat.shape[0])
    c = lat[:, :MLA_KV_LORA]

    def block_fn(qb, pos):
        logits = jnp.einsum('thc,lc->htl', qb, lat).astype(F32) * MLA_SCALE
        p, _ = _softmax_lse(logits, (kpos[None, :] <= pos[:, None])[None], -1)
        return jnp.einsum('htl,lc->thc', p.astype(c.dtype), c)

    return _map_query_blocks(block_fn, (q,), q_pos, DENSE_Q_BLOCK)


def _mixer_mla(hp, hs, cache_d_latent, page_table, pos_p, pos_s,
               w_qa, g_qn, w_qb, w_kva, g_kvn, w_kvb, w_o):
    w_kvb_r = w_kvb.reshape(MLA_KV_LORA, N_HEADS, MLA_NOPE + MLA_V)
    w_uk = w_kvb_r[..., :MLA_NOPE]
    w_uv = w_kvb_r[..., MLA_NOPE:]

    def proj(h, pos):
        B, T, _ = h.shape
        q = (_rmsnorm(h @ w_qa, g_qn) @ w_qb).reshape(B, T, N_HEADS, MLA_NOPE + MLA_ROPE)
        q_pe = _rope(q[..., MLA_NOPE:], pos)
        q_lat = jnp.einsum('bthn,chn->bthc', q[..., :MLA_NOPE], w_uk)
        kv = h @ w_kva
        c = _rmsnorm(kv[..., :MLA_KV_LORA], g_kvn)
        k_pe = _rope(kv[..., None, MLA_KV_LORA:], pos)[..., 0, :]
        return jnp.concatenate([q_lat, q_pe], -1), jnp.concatenate([c, k_pe], -1)
    qp, latp = proj(hp, pos_p)
    qs, lats = proj(hs, pos_s)
    op = lax.map(lambda a: _mla_seq(a[0], a[1], pos_p), (qp, latp))

    def sample_seq(a):
        q, lat_new, pt = a
        lat = jnp.concatenate([_gather_pages(cache_d_latent, pt), lat_new], axis=0)
        return _mla_seq(q, lat, pos_s)
    os_ = lax.map(sample_seq, (qs, lats, page_table))

    def out(o, h):
        return jnp.einsum('bthc,chv->bthv', o, w_uv).reshape(h.shape[:2] + (N_HEADS * MLA_V,)) @ w_o
    return out(op, hp), out(os_, hs), latp, lats


def setup_inputs(seed: int = 0) -> dict:
    key = jax.random.key(seed)
    keys = jax.random.split(key, 40)

    def nrm(i, shape, scale):
        return scale * jax.random.normal(keys[i], shape, F32)

    n_pages = PAST_LEN // PAGE_SIZE
    n_used = DEC_BATCH * n_pages
    n_pool = n_used + max(1, n_used // 4)
    page_table = jax.random.permutation(keys[0], n_pool)[:n_used].reshape(DEC_BATCH, n_pages).astype(jnp.int32)
    D = D_MODEL
    return {
        'x_prompt': nrm(1, (BATCH, SEQ, D), 1.0),
        'x_sample': nrm(2, (DEC_BATCH, DEC_SEQ, D), 1.0),
        'cache_a_kv': nrm(3, (n_pool, PAGE_SIZE, 2, MOBA_KV_HEADS, HEAD_DIM), 1.0),
        'state_b_kv1': nrm(4, (DEC_BATCH, min(DIL_PATTERNS[0][0], PAST_LEN), 2, DIL_KV_HEADS, HEAD_DIM), 1.0),
        'state_b_kv2': nrm(5, (DEC_BATCH, min(DIL_PATTERNS[1][0], PAST_LEN), 2, DIL_KV_HEADS, HEAD_DIM), 1.0),
        'state_b_kv3': nrm(6, (DEC_BATCH, min(DIL_PATTERNS[2][0], PAST_LEN), 2, DIL_KV_HEADS, HEAD_DIM), 1.0),
        'cache_c_kv': nrm(7, (n_pool, PAGE_SIZE, 4, HEAD_DIM), 1.0),
        'state_c_win': nrm(8, (DEC_BATCH, min(WIN, PAST_LEN), 2, HEAD_DIM), 1.0),
        'cache_d_latent': nrm(9, (n_pool, PAGE_SIZE, MLA_KV_LORA + MLA_ROPE), 1.0),
        'page_table': page_table,
        'rel_bias': nrm(10, (REL_BUCKETS, N_HEADS), 0.2),
        'g_attn_norm': 1.0 + nrm(11, (DEPTH, D), 0.02),
        'g_ffn_norm': 1.0 + nrm(12, (DEPTH, D), 0.02),
        'g_final_norm': 1.0 + nrm(13, (D,), 0.02),
        'w_a_qkv': nrm(14, (D, D_ATTN + 2 * MOBA_KV_HEADS * HEAD_DIM), D ** -0.5),
        'w_a_o': nrm(15, (D_ATTN, D), D_ATTN ** -0.5),
        'w_b_qkv': nrm(16, (D, len(DIL_PATTERNS) * (D_ATTN + 2 * DIL_KV_HEADS * HEAD_DIM)), D ** -0.5),
        'w_b_o': nrm(17, (D_ATTN, D), D_ATTN ** -0.5),
        'w_c_qkv': nrm(18, (D, D_ATTN + NSA_KV_ROWS * HEAD_DIM), D ** -0.5),
        'w_c_gate': nrm(19, (D, 3 * N_HEADS), D ** -0.5),
        'w_c_cmp1': nrm(20, (2, CMP_BLOCK * HEAD_DIM, CMP_HIDDEN), (CMP_BLOCK * HEAD_DIM) ** -0.5),
        'w_c_cmp2': nrm(21, (2, CMP_HIDDEN, HEAD_DIM), CMP_HIDDEN ** -0.5),
        'w_c_cmp_pe': nrm(22, (2, CMP_BLOCK, HEAD_DIM), 0.02),
        'w_c_o': nrm(23, (D_ATTN, D), D_ATTN ** -0.5),
        'w_d_qa': nrm(24, (D, MLA_Q_LORA), D ** -0.5),
        'g_d_qnorm': 1.0 + nrm(25, (MLA_Q_LORA,), 0.02),
        'w_d_qb': nrm(26, (MLA_Q_LORA, N_HEADS * (MLA_NOPE + MLA_ROPE)), MLA_Q_LORA ** -0.5),
        'w_d_kva': nrm(27, (D, MLA_KV_LORA + MLA_ROPE), D ** -0.5),
        'g_d_kvnorm': 1.0 + nrm(28, (MLA_KV_LORA,), 0.02),
        'w_d_kvb': nrm(29, (MLA_KV_LORA, N_HEADS * (MLA_NOPE + MLA_V)), MLA_KV_LORA ** -0.5),
        'w_d_o': nrm(30, (N_HEADS * MLA_V, D), (N_HEADS * MLA_V) ** -0.5),
        'w_ffn_in': nrm(31, (DEPTH, D, 2 * D_FF), D ** -0.5),
        'w_ffn_out': nrm(32, (DEPTH, D_FF, D), D_FF ** -0.5),
    }


def reference(x_prompt, x_sample, cache_a_kv, state_b_kv1, state_b_kv2, state_b_kv3, cache_c_kv, state_c_win,
              cache_d_latent, page_table, rel_bias, g_attn_norm, g_ffn_norm, g_final_norm,
              w_a_qkv, w_a_o, w_b_qkv, w_b_o, w_c_qkv, w_c_gate, w_c_cmp1, w_c_cmp2, w_c_cmp_pe, w_c_o,
              w_d_qa, g_d_qnorm, w_d_qb, w_d_kva, g_d_kvnorm, w_d_kvb, w_d_o, w_ffn_in, w_ffn_out):
    pos_p = jnp.arange(x_prompt.shape[1], dtype=jnp.int32)
    pos_s = PAST_LEN + jnp.arange(x_sample.shape[1], dtype=jnp.int32)
    xp, xs = x_prompt, x_sample
    for layer in range(DEPTH):
        hp = _rmsnorm(xp, g_attn_norm[layer])
        hs = _rmsnorm(xs, g_attn_norm[layer])
        kind = layer % N_MIXERS
        if kind == 0:
            mp, ms, a_kv_prompt, a_kv_sample = _mixer_moba(
                hp, hs, cache_a_kv, page_table, pos_p, pos_s, rel_bias, w_a_qkv, w_a_o)
        elif kind == 1:
            mp, ms, b_new_p, b_new_s = _mixer_dilated(
                hp, hs, state_b_kv1, state_b_kv2, state_b_kv3, rel_bias, w_b_qkv, w_b_o)
            b_kv1_prompt, b_kv2_prompt, b_kv3_prompt = b_new_p
            b_kv1_sample, b_kv2_sample, b_kv3_sample = b_new_s
        elif kind == 2:
            mp, ms, c_kv_prompt, c_kv_sample, c_win_prompt, c_win_sample = _mixer_nsa(
                hp, hs, cache_c_kv, state_c_win, page_table, pos_p, pos_s, rel_bias,
                w_c_qkv, w_c_gate, w_c_cmp1, w_c_cmp2, w_c_cmp_pe, w_c_o)
        else:
            mp, ms, d_latent_prompt, d_latent_sample = _mixer_mla(
                hp, hs, cache_d_latent, page_table, pos_p, pos_s,
                w_d_qa, g_d_qnorm, w_d_qb, w_d_kva, g_d_kvnorm, w_d_kvb, w_d_o)
        xp = xp + mp.astype(xp.dtype)
        xs = xs + ms.astype(xs.dtype)
        xp = xp + _swiglu(_rmsnorm(xp, g_ffn_norm[layer]), w_ffn_in[layer], w_ffn_out[layer])
        xs = xs + _swiglu(_rmsnorm(xs, g_ffn_norm[layer]), w_ffn_in[layer], w_ffn_out[layer])
    y_prompt = _rmsnorm(xp, g_final_norm)
    y_sample = _rmsnorm(xs, g_final_norm)
    return (y_prompt, y_sample, a_kv_prompt, a_kv_sample, b_kv1_prompt, b_kv1_sample, b_kv2_prompt, b_kv2_sample,
            b_kv3_prompt, b_kv3_sample, c_kv_prompt, c_kv_sample, c_win_prompt, c_win_sample,
            d_latent_prompt, d_latent_sample)
```

```python
import functools
import math

import jax
import jax.numpy as jnp
import numpy as np
from jax import lax
from jax.experimental import pallas as pl
from jax.experimental.pallas import tpu as pltpu

F32 = jnp.float32
BF16 = jnp.bfloat16

D_MODEL = 2048
DEPTH = 4
PAST_LEN = 8192
HEAD_DIM = 128
N_HEADS = 16
D_ATTN = N_HEADS * HEAD_DIM
ATTN_SCALE = HEAD_DIM ** -0.5
D_FF = 5632
RMS_EPS = 1e-6
REL_BUCKETS = 32
REL_MAX_DIST = 128
NEG_INF = -1e30
FORCE_SCORE = 1e30
GATHER_Q_BLOCK = 16
DENSE_Q_BLOCK = 128
MOBA_BLOCK = 256
MOBA_TOPK = 3
MOBA_KV_HEADS = 2
DIL_PATTERNS = ((128, 1), (512, 4), (2048, 16))
DIL_KV_HEADS = 4
CMP_BLOCK = 32
CMP_STRIDE = 16
SEL_BLOCK = 64
SEL_TOPK = 16
WIN = 512
WIN_BLOCK = 128
NSA_KV_ROWS = 6
MLA_Q_LORA = 512
MLA_KV_LORA = 512
MLA_NOPE = 128
MLA_ROPE = 64
MLA_V = 128
MLA_SCALE = (MLA_NOPE + MLA_ROPE) ** -0.5
ROPE_THETA = 10000.0

VMEM_LIMIT_BYTES = 56 * 1024 * 1024
LANE = 128


def _compiler_params(n_grid_axes):
    return pltpu.CompilerParams(
        dimension_semantics=("arbitrary",) * n_grid_axes,
        vmem_limit_bytes=VMEM_LIMIT_BYTES,
    )


def _rmsnorm_kernel(x_ref, g_ref, o_ref):
    x = x_ref[...]
    ms = jnp.mean(x * x, axis=-1, keepdims=True)
    o_ref[...] = ((x * lax.rsqrt(ms + RMS_EPS)) * g_ref[...]).astype(o_ref.dtype)


def _rmsnorm(x, g, out_dtype, tm):
    n, d = x.shape
    return pl.pallas_call(
        _rmsnorm_kernel,
        grid=(n // tm,),
        in_specs=[pl.BlockSpec((tm, d), lambda i: (i, 0)), pl.BlockSpec((1, d), lambda i: (0, 0))],
        out_specs=pl.BlockSpec((tm, d), lambda i: (i, 0)),
        out_shape=jax.ShapeDtypeStruct((n, d), out_dtype),
        compiler_params=_compiler_params(1),
        name="rmsnorm",
    )(x, g.reshape(1, d))


def _mm_kernel(*refs, has_res, cast_w):
    x_ref, w_ref = refs[0], refs[1]
    rest = refs[2:]
    if has_res:
        r_ref, rest = rest[0], rest[1:]
    o_ref = rest[0]
    if cast_w:
        wb_ref = rest[1]

        @pl.when(pl.program_id(1) == 0)
        def _():
            wb_ref[...] = w_ref[...].astype(BF16)

        w = wb_ref[...]
    else:
        w = w_ref[...]
    acc = jnp.dot(x_ref[...].astype(BF16), w, preferred_element_type=F32)
    if has_res:
        acc = r_ref[...] + acc
    o_ref[...] = acc.astype(o_ref.dtype)


def _matmul(x, w, *, tm, tn, res=None, out_dtype=F32, w_layer=None, w_col0=0, n_out=None):
    m, k = x.shape
    n = n_out if n_out is not None else w.shape[-1]
    assert m % tm == 0 and n % tn == 0 and w_col0 % tn == 0
    cast_w = w.dtype != BF16
    cb = w_col0 // tn
    if w_layer is None:
        w_spec = pl.BlockSpec((k, tn), lambda j, i: (0, j + cb))
    else:
        w_spec = pl.BlockSpec((None, k, tn), lambda j, i: (w_layer, 0, j + cb))
    in_specs = [pl.BlockSpec((tm, k), lambda j, i: (i, 0)), w_spec]
    args = [x, w]
    if res is not None:
        in_specs.append(pl.BlockSpec((tm, tn), lambda j, i: (i, j)))
        args.append(res)
    return pl.pallas_call(
        functools.partial(_mm_kernel, has_res=res is not None, cast_w=cast_w),
        grid=(n // tn, m // tm),
        in_specs=in_specs,
        out_specs=pl.BlockSpec((tm, tn), lambda j, i: (i, j)),
        out_shape=jax.ShapeDtypeStruct((m, n), out_dtype),
        scratch_shapes=[pltpu.VMEM((k, tn), BF16)] if cast_w else [],
        compiler_params=_compiler_params(2),
        name="matmul",
    )(*args)


def _ffn_in_kernel(x_ref, wg_ref, wu_ref, o_ref, wgb_ref, wub_ref):
    @pl.when(pl.program_id(1) == 0)
    def _():
        wgb_ref[...] = wg_ref[...].astype(BF16)
        wub_ref[...] = wu_ref[...].astype(BF16)

    x = x_ref[...]
    g = jnp.dot(x, wgb_ref[...], preferred_element_type=F32)
    u = jnp.dot(x, wub_ref[...], preferred_element_type=F32)
    o_ref[...] = ((g * jax.nn.sigmoid(g)) * u).astype(o_ref.dtype)


def _ffn_in(h, w_ffn_in, layer, *, tm, tf):
    m, k = h.shape
    nf = D_FF // tf
    return pl.pallas_call(
        _ffn_in_kernel,
        grid=(nf, m // tm),
        in_specs=[
            pl.BlockSpec((tm, k), lambda j, i: (i, 0)),
            pl.BlockSpec((None, k, tf), lambda j, i: (layer, 0, j)),
            pl.BlockSpec((None, k, tf), lambda j, i: (layer, 0, j + nf)),
        ],
        out_specs=pl.BlockSpec((tm, tf), lambda j, i: (i, j)),
        out_shape=jax.ShapeDtypeStruct((m, D_FF), BF16),
        scratch_shapes=[pltpu.VMEM((k, tf), BF16), pltpu.VMEM((k, tf), BF16)],
        compiler_params=_compiler_params(2),
        name="ffn_in",
    )(h, w_ffn_in, w_ffn_in)


def _rel_bucket(dist):
    n = jnp.maximum(dist, 0)
    max_exact = REL_BUCKETS // 2
    large = max_exact + (jnp.log(jnp.maximum(n, 1).astype(F32) / max_exact)
                         / math.log(REL_MAX_DIST / max_exact) * (REL_BUCKETS - max_exact)).astype(jnp.int32)
    large = jnp.minimum(large, REL_BUCKETS - 1)
    return jnp.where(n < max_exact, n, large)


def _softmax_lse(logits, mask, axis):
    l = jnp.where(mask, logits.astype(F32), NEG_INF)
    m = jnp.max(l, axis=axis, keepdims=True)
    e = jnp.where(mask, jnp.exp(l - m), 0.0)
    s = jnp.sum(e, axis=axis, keepdims=True)
    s = jnp.maximum(s, 1e-30)
    return e / s, jnp.squeeze(m + jnp.log(s), axis)


def _rope(x, pos):
    half = x.shape[-1] // 2
    inv = ROPE_THETA ** (-jnp.arange(half, dtype=F32) / half)
    ang = pos.astype(F32)[:, None] * inv[None, :]
    cos = jnp.cos(ang)[:, None, :]
    sin = jnp.sin(ang)[:, None, :]
    xf = x.astype(F32)
    x1, x2 = xf[..., :half], xf[..., half:]
    return jnp.concatenate([x1 * cos - x2 * sin, x1 * sin + x2 * cos], -1).astype(x.dtype)


def _rmsnorm_jax(x, g):
    xf = x.astype(F32)
    y = xf * lax.rsqrt(jnp.mean(xf * xf, axis=-1, keepdims=True) + RMS_EPS)
    return (y * g.astype(F32)).astype(x.dtype)


def _gather_pages(pool, pt_row):
    g = pool[pt_row]
    return g.reshape((-1,) + pool.shape[2:])


def _map_query_blocks(fn, q_arrays, q_pos, block):
    T = q_pos.shape[0]
    if T <= block or T % block:
        return fn(*q_arrays, q_pos)
    nb = T // block
    xs = tuple(a.reshape((nb, block) + a.shape[1:]) for a in q_arrays) + (q_pos.reshape(nb, block),)
    out = lax.map(lambda a: fn(*a), xs)
    return out.reshape((T,) + out.shape[2:])


def _moba_seq(q, kv, q_pos, rel_bias):
    L = kv.shape[0]
    nb = -(-L // MOBA_BLOCK)
    kv = jnp.pad(kv, ((0, nb * MOBA_BLOCK - L), (0, 0), (0, 0), (0, 0)))
    kvb = kv.reshape(nb, MOBA_BLOCK, 2, MOBA_KV_HEADS, HEAD_DIM).transpose(2, 3, 0, 1, 4)
    kb, vb = kvb[0], kvb[1]
    kmean = jnp.mean(kb.astype(F32), axis=2)
    topk = min(MOBA_TOPK, nb)
    G = N_HEADS // MOBA_KV_HEADS
    tb = rel_bias.T.reshape(MOBA_KV_HEADS, G, REL_BUCKETS)
    kv_i = jnp.arange(MOBA_KV_HEADS)[None, :, None, None]
    g_i = jnp.arange(G)[None, None, :, None, None]

    def block_fn(qb, pos):
        Tb = qb.shape[0]
        qg = qb.reshape(Tb, MOBA_KV_HEADS, G, HEAD_DIM)
        own = pos // MOBA_BLOCK
        gate = jnp.einsum('tkgd,knd->tkgn', qg.astype(F32), kmean)
        past = jnp.arange(nb)[None, :] < own[:, None]
        gate = jnp.where(past[:, None, None, :], gate, NEG_INF)
        _, sel = lax.top_k(gate, topk)
        sel_ok = sel < own[:, None, None, None]
        own_b = jnp.broadcast_to(own[:, None, None, None], (Tb, MOBA_KV_HEADS, G, 1))
        blocks = jnp.concatenate([sel, own_b], -1)
        ok_blk = jnp.concatenate([sel_ok, jnp.ones_like(own_b, dtype=bool)], -1)
        kg = kb[kv_i, blocks]
        vg = vb[kv_i, blocks]
        kpos = blocks[..., None] * MOBA_BLOCK + jnp.arange(MOBA_BLOCK)
        dist = pos[:, None, None, None, None] - kpos
        logits = (jnp.einsum('tkgd,tkgnsd->tkgns', qg, kg).astype(F32) * ATTN_SCALE
                  + tb[kv_i[..., None], g_i, _rel_bucket(dist)])
        mask = ok_blk[..., None] & (dist >= 0)
        nk = (topk + 1) * MOBA_BLOCK
        p, _ = _softmax_lse(logits.reshape(Tb, MOBA_KV_HEADS, G, nk), mask.reshape(Tb, MOBA_KV_HEADS, G, nk), -1)
        out = jnp.einsum('tkgj,tkgjd->tkgd', p.astype(vg.dtype), vg.reshape(Tb, MOBA_KV_HEADS, G, nk, HEAD_DIM))
        return out.reshape(Tb, N_HEADS, HEAD_DIM)

    return _map_query_blocks(block_fn, (q,), q_pos, GATHER_Q_BLOCK)


def _moba_core(qp, kvp, qs, kvs, cache_a_kv, page_table, pos_p, pos_s, rel_bias):
    op = lax.map(lambda a: _moba_seq(a[0], a[1], pos_p, rel_bias), (qp, kvp))

    def sample_seq(a):
        q, kv_new, pt = a
        kv_all = jnp.concatenate([_gather_pages(cache_a_kv, pt), kv_new], axis=0)
        return _moba_seq(q, kv_all, pos_s, rel_bias)
    os_ = lax.map(sample_seq, (qs, kvs, page_table))
    return op, os_


def _dilated_prompt(q, kv, win, dil, rel_bias):
    B, S = q.shape[:2]
    R = win // dil
    Sp = -(-S // win) * win
    nbu = Sp // win
    G = N_HEADS // DIL_KV_HEADS
    q = jnp.pad(q, ((0, 0), (0, Sp - S), (0, 0), (0, 0)))
    kv = jnp.pad(kv, ((0, 0), (0, Sp - S), (0, 0), (0, 0), (0, 0)))
    qr = q.reshape(B, nbu, R, dil, DIL_KV_HEADS, G, HEAD_DIM)
    kvr = kv.reshape(B, nbu, R, dil, 2, DIL_KV_HEADS, HEAD_DIM)
    prev = jnp.pad(kvr, ((0, 0), (1, 0), (0, 0), (0, 0), (0, 0), (0, 0), (0, 0)))[:, :nbu]
    kk = jnp.concatenate([prev, kvr], axis=2)
    a = jnp.arange(R)[:, None]
    j = jnp.arange(2 * R)[None, :]
    steps = R + a - j
    band = (steps >= 0) & (steps <= R)
    kvalid = (jnp.arange(nbu)[:, None] > 0) | (jnp.arange(2 * R)[None, :] >= R)
    ok = band[None] & kvalid[:, None, :]
    bias = rel_bias[_rel_bucket(steps * dil)].transpose(2, 0, 1).reshape(DIL_KV_HEADS, G, R, 2 * R)
    logits = jnp.einsum('bnqrkgd,bnjrkd->bnrkgqj', qr, kk[:, :, :, :, 0]).astype(F32) * ATTN_SCALE + bias
    p, lse = _softmax_lse(logits, ok[None, :, None, None, None, :, :], -1)
    out = jnp.einsum('bnrkgqj,bnjrkd->bnqrkgd', p.astype(kk.dtype), kk[:, :, :, :, 1])
    out = out.reshape(B, Sp, N_HEADS, HEAD_DIM)[:, :S]
    lse = lse.transpose(0, 1, 5, 2, 3, 4).reshape(B, Sp, N_HEADS)[:, :S]
    return out, lse


def _dilated_sample(q, kv_new, buf, win, dil, rel_bias):
    DB, T = q.shape[:2]
    Lbuf = buf.shape[1]
    R = win // dil
    G = N_HEADS // DIL_KV_HEADS
    ext = jnp.concatenate([buf, kv_new], axis=1)
    m = jnp.arange(R + 1)
    idx = Lbuf + jnp.arange(T)[:, None] - m[None, :] * dil
    ok = idx >= 0
    g = ext[:, jnp.clip(idx, 0)]
    qg = q.reshape(DB, T, DIL_KV_HEADS, G, HEAD_DIM)
    bias = rel_bias[_rel_bucket(m * dil)].T.reshape(DIL_KV_HEADS, G, R + 1)
    logits = jnp.einsum('btkgd,btjkd->btkgj', qg, g[:, :, :, 0]).astype(F32) * ATTN_SCALE + bias
    p, lse = _softmax_lse(logits, ok[None, :, None, None, :], -1)
    out = jnp.einsum('btkgj,btjkd->btkgd', p.astype(g.dtype), g[:, :, :, 1])
    return out.reshape(DB, T, N_HEADS, HEAD_DIM), lse.reshape(DB, T, N_HEADS), ext[:, T:]


def _merge_groups(outs, lses):
    w = jax.nn.softmax(jnp.stack(lses, 0), axis=0)
    o = jnp.sum(w[..., None] * jnp.stack(outs, 0).astype(F32), axis=0)
    B, T = o.shape[:2]
    return o.reshape(B, T, D_ATTN).astype(outs[0].dtype)


def _dilated_core(qp, kvp, qs, kvs, bufs, rel_bias):
    S = qp.shape[1]
    outs_p, lses_p, outs_s, lses_s, new_p, new_s = [], [], [], [], [], []
    for gi, (win, dil) in enumerate(DIL_PATTERNS):
        o, l = _dilated_prompt(qp[:, :, gi], kvp[:, :, gi], win, dil, rel_bias)
        outs_p.append(o)
        lses_p.append(l)
        new_p.append(kvp[:, S - min(win, S):, gi])
        o, l, nbuf = _dilated_sample(qs[:, :, gi], kvs[:, :, gi], bufs[gi], win, dil, rel_bias)
        outs_s.append(o)
        lses_s.append(l)
        new_s.append(nbuf)
    return _merge_groups(outs_p, lses_p), _merge_groups(outs_s, lses_s), new_p, new_s


def _compress(kv2, w1, w2, pe):
    L = kv2.shape[0]
    n_cmp = (L - CMP_BLOCK) // CMP_STRIDE + 1
    idx = jnp.arange(n_cmp)[:, None] * CMP_STRIDE + jnp.arange(CMP_BLOCK)[None, :]
    blk = kv2[idx].transpose(2, 0, 1, 3) + pe[:, None]
    hid = jax.nn.gelu(jnp.einsum('cnx,cxh->cnh', blk.reshape(2, n_cmp, CMP_BLOCK * HEAD_DIM), w1))
    return jnp.einsum('cnh,chd->cnd', hid, w2)


def _nsa_seq(q, gates, kv4, q_pos, rel_bias, w_cmp1, w_cmp2, w_cmp_pe):
    L = kv4.shape[0]
    cmp = _compress(kv4[:, 0:2], w_cmp1, w_cmp2, w_cmp_pe)
    kc, vc = cmp[0], cmp[1]
    n_cmp = kc.shape[0]
    tok_start = jnp.arange(n_cmp) * CMP_STRIDE
    cmp_end = tok_start + CMP_BLOCK - 1
    n_sel = -(-L // SEL_BLOCK)
    sel_kv = jnp.pad(kv4[:, 2:4], ((0, n_sel * SEL_BLOCK - L), (0, 0), (0, 0))).reshape(n_sel, SEL_BLOCK, 2, HEAD_DIM)
    blk_start = jnp.arange(n_sel) * SEL_BLOCK
    overlap = ((tok_start[:, None] < blk_start[None, :] + SEL_BLOCK)
               & (tok_start[:, None] + CMP_BLOCK > blk_start[None, :])).astype(F32)
    n_top = min(SEL_TOPK, n_sel)

    def block_fn(qb, gb, pos):
        Tb = qb.shape[0]
        dc = pos[:, None] - cmp_end[None, :]
        lc = (jnp.einsum('thd,nd->thn', qb, kc).astype(F32) * ATTN_SCALE
              + rel_bias[_rel_bucket(dc)].transpose(0, 2, 1))
        pc, _ = _softmax_lse(lc, (dc >= 0)[:, None, :], -1)
        o_cmp = jnp.einsum('thn,nd->thd', pc.astype(vc.dtype), vc)
        imp = jnp.sum(pc, axis=1) @ overlap
        cur = pos // SEL_BLOCK
        j = jnp.arange(n_sel)[None, :]
        forced = (j == 0) | (j == cur[:, None]) | (j == cur[:, None] - 1)
        score = jnp.where(j > cur[:, None], NEG_INF, jnp.where(forced, FORCE_SCORE, imp))
        _, sidx = lax.top_k(score, n_top)
        g = sel_kv[sidx]
        kpos = sidx[..., None] * SEL_BLOCK + jnp.arange(SEL_BLOCK)
        ds = pos[:, None, None] - kpos
        ls = (jnp.einsum('thd,tnsd->thns', qb, g[..., 0, :]).astype(F32) * ATTN_SCALE
              + rel_bias[_rel_bucket(ds)].transpose(0, 3, 1, 2))
        nk = n_top * SEL_BLOCK
        ps, _ = _softmax_lse(ls.reshape(Tb, N_HEADS, nk), (ds >= 0).reshape(Tb, 1, nk), -1)
        o_sel = jnp.einsum('thj,tjd->thd', ps.astype(g.dtype), g[..., 1, :].reshape(Tb, nk, HEAD_DIM))
        return gb[..., 0:1] * o_cmp + gb[..., 1:2] * o_sel

    return _map_query_blocks(block_fn, (q, gates), q_pos, GATHER_Q_BLOCK)


def _window_prompt(q, kvw, rel_bias):
    B, S = q.shape[:2]
    P = WIN // WIN_BLOCK
    nb = S // WIN_BLOCK
    J = (P + 1) * WIN_BLOCK
    qr = q.reshape(B, nb, WIN_BLOCK, N_HEADS, HEAD_DIM)
    kvr = jnp.pad(kvw.reshape(B, nb, WIN_BLOCK, 2, HEAD_DIM), ((0, 0), (P, 0), (0, 0), (0, 0), (0, 0)))
    kk = jnp.concatenate([kvr[:, i:i + nb] for i in range(P + 1)], axis=2)
    dist = P * WIN_BLOCK + jnp.arange(WIN_BLOCK)[:, None] - jnp.arange(J)[None, :]
    kpos = jnp.arange(nb)[:, None] * WIN_BLOCK - P * WIN_BLOCK + jnp.arange(J)[None, :]
    ok = ((dist >= 0) & (dist <= WIN))[None] & (kpos >= 0)[:, None, :]
    bias = rel_bias[_rel_bucket(dist)].transpose(2, 0, 1)
    logits = jnp.einsum('bnqhd,bnjd->bnhqj', qr, kk[:, :, :, 0]).astype(F32) * ATTN_SCALE + bias
    p, _ = _softmax_lse(logits, ok[None, :, None], -1)
    out = jnp.einsum('bnhqj,bnjd->bnqhd', p.astype(kk.dtype), kk[:, :, :, 1])
    return out.reshape(B, S, N_HEADS, HEAD_DIM)


def _window_sample(q, kvw_new, buf, q_pos, rel_bias):
    T = q.shape[1]
    Lbuf = buf.shape[1]
    ext = jnp.concatenate([buf, kvw_new], axis=1)
    kpos = PAST_LEN - Lbuf + jnp.arange(Lbuf + T)
    dist = q_pos[:, None] - kpos[None, :]
    ok = (dist >= 0) & (dist <= WIN)
    bias = rel_bias[_rel_bucket(dist)].transpose(2, 0, 1)
    logits = jnp.einsum('bthd,bjd->bhtj', q, ext[:, :, 0]).astype(F32) * ATTN_SCALE + bias
    p, _ = _softmax_lse(logits, ok[None, None], -1)
    out = jnp.einsum('bhtj,bjd->bthd', p.astype(ext.dtype), ext[:, :, 1])
    return out, ext[:, T:]


def _nsa_core(qp, kvp, gp, qs, kvs, gs, cache_c_kv, state_c_win, page_table, pos_p, pos_s, rel_bias,
              w_cmp1, w_cmp2, w_cmp_pe):
    op = lax.map(lambda a: _nsa_seq(a[0], a[1], a[2][:, :4], pos_p, rel_bias, w_cmp1, w_cmp2, w_cmp_pe),
                 (qp, gp, kvp))

    def sample_seq(a):
        q, g, kv_new, pt = a
        kv_all = jnp.concatenate([_gather_pages(cache_c_kv, pt), kv_new[:, :4]], axis=0)
        return _nsa_seq(q, g, kv_all, pos_s, rel_bias, w_cmp1, w_cmp2, w_cmp_pe)
    os_ = lax.map(sample_seq, (qs, gs, kvs, page_table))
    op = op + gp[..., 2:3] * _window_prompt(qp, kvp[:, :, 4:6], rel_bias)
    ow, win_s = _window_sample(qs, kvs[:, :, 4:6], state_c_win, pos_s, rel_bias)
    os_ = os_ + gs[..., 2:3] * ow
    return op, os_, win_s


def _mla_seq(q, lat, q_pos):
    kpos = jnp.arange(lat.shape[0])
    c = lat[:, :MLA_KV_LORA]

    def block_fn(qb, pos):
        logits = jnp.einsum('thc,lc->htl', qb, lat).astype(F32) * MLA_SCALE
        p, _ = _softmax_lse(logits, (kpos[None, :] <= pos[:, None])[None], -1)
        return jnp.einsum('htl,lc->thc', p.astype(c.dtype), c)

    return _map_query_blocks(block_fn, (q,), q_pos, DENSE_Q_BLOCK)


def kernel(x_prompt, x_sample, cache_a_kv, state_b_kv1, state_b_kv2, state_b_kv3, cache_c_kv, state_c_win,
           cache_d_latent, page_table, rel_bias, g_attn_norm, g_ffn_norm, g_final_norm,
           w_a_qkv, w_a_o, w_b_qkv, w_b_o, w_c_qkv, w_c_gate, w_c_cmp1, w_c_cmp2, w_c_cmp_pe, w_c_o,
           w_d_qa, g_d_qnorm, w_d_qb, w_d_kva, g_d_kvnorm, w_d_kvb, w_d_o, w_ffn_in, w_ffn_out):
    B, S, D = x_prompt.shape
    DB, T, _ = x_sample.shape
    NP = B * S
    NS = DB * T
    N = NP + NS
    TM = 640
    assert N % TM == 0
    pos_p = jnp.arange(S, dtype=jnp.int32)
    pos_s = PAST_LEN + jnp.arange(T, dtype=jnp.int32)

    x = jnp.concatenate([x_prompt.reshape(NP, D), x_sample.reshape(NS, D)], axis=0)
    w_ffn_out_bf = w_ffn_out.astype(BF16)

    def split(a):
        return a[:NP].reshape((B, S) + a.shape[1:]), a[NP:].reshape((DB, T) + a.shape[1:])

    outs = {}
    for layer in range(DEPTH):
        h = _rmsnorm(x, g_attn_norm[layer], BF16, TM)
        kind = layer % 4
        if kind == 0:
            qkv = _matmul(h, w_a_qkv, tm=TM, tn=512)
            q = qkv[:, :D_ATTN].reshape(N, N_HEADS, HEAD_DIM)
            kv = qkv[:, D_ATTN:].reshape(N, 2, MOBA_KV_HEADS, HEAD_DIM)
            qp, qs = split(q)
            kvp, kvs = split(kv)
            op, os_ = _moba_core(qp, kvp, qs, kvs, cache_a_kv, page_table, pos_p, pos_s, rel_bias)
            o = jnp.concatenate([op.reshape(NP, D_ATTN), os_.reshape(NS, D_ATTN)], axis=0)
            x = _matmul(o, w_a_o, tm=TM, tn=512, res=x)
            outs['a_kv'] = (kvp, kvs)
        elif kind == 1:
            n_g = len(DIL_PATTERNS)
            width = D_ATTN + 2 * DIL_KV_HEADS * HEAD_DIM
            qkv = _matmul(h, w_b_qkv, tm=TM, tn=512).reshape(N, n_g, width)
            q = qkv[..., :D_ATTN].reshape(N, n_g, N_HEADS, HEAD_DIM)
            kv = qkv[..., D_ATTN:].reshape(N, n_g, 2, DIL_KV_HEADS, HEAD_DIM)
            qp, qs = split(q)
            kvp, kvs = split(kv)
            op, os_, new_p, new_s = _dilated_core(qp, kvp, qs, kvs, (state_b_kv1, state_b_kv2, state_b_kv3),
                                                  rel_bias)
            o = jnp.concatenate([op.reshape(NP, D_ATTN), os_.reshape(NS, D_ATTN)], axis=0)
            x = _matmul(o, w_b_o, tm=TM, tn=512, res=x)
            outs['b'] = (new_p, new_s)
        elif kind == 2:
            qkv = _matmul(h, w_c_qkv, tm=TM, tn=256)
            gl = _matmul(h, w_c_gate, tm=TM, tn=3 * N_HEADS)
            q = qkv[:, :D_ATTN].reshape(N, N_HEADS, HEAD_DIM)
            kv = qkv[:, D_ATTN:].reshape(N, NSA_KV_ROWS, HEAD_DIM)
            gates = jax.nn.sigmoid(gl).reshape(N, N_HEADS, 3)
            qp, qs = split(q)
            kvp, kvs = split(kv)
            gp, gs = split(gates)
            op, os_, win_s = _nsa_core(qp, kvp, gp, qs, kvs, gs, cache_c_kv, state_c_win, page_table,
                                       pos_p, pos_s, rel_bias, w_c_cmp1, w_c_cmp2, w_c_cmp_pe)
            o = jnp.concatenate([op.reshape(NP, D_ATTN), os_.reshape(NS, D_ATTN)], axis=0)
            x = _matmul(o, w_c_o, tm=TM, tn=512, res=x)
            outs['c'] = (kvp[:, :, :4], kvs[:, :, :4], kvp[:, S - min(WIN, S):, 4:6], win_s)
        else:
            w_kvb_r = w_d_kvb.reshape(MLA_KV_LORA, N_HEADS, MLA_NOPE + MLA_V)
            w_uk = w_kvb_r[..., :MLA_NOPE]
            w_uv = w_kvb_r[..., MLA_NOPE:]
            qa = _matmul(h, w_d_qa, tm=TM, tn=512)
            qn = _rmsnorm(qa, g_d_qnorm, BF16, TM)
            qfull = _matmul(qn, w_d_qb, tm=TM, tn=512).reshape(N, N_HEADS, MLA_NOPE + MLA_ROPE)
            kva = _matmul(h, w_d_kva, tm=TM, tn=MLA_KV_LORA + MLA_ROPE)
            pos_all = jnp.concatenate([jnp.tile(pos_p, B), jnp.tile(pos_s, DB)])

            def rope_rows(xr):
                half = xr.shape[-1] // 2
                inv = ROPE_THETA ** (-jnp.arange(half, dtype=F32) / half)
                ang = pos_all.astype(F32)[:, None] * inv[None, :]
                cos = jnp.cos(ang)[:, None, :]
                sin = jnp.sin(ang)[:, None, :]
                x1, x2 = xr[..., :half], xr[..., half:]
                return jnp.concatenate([x1 * cos - x2 * sin, x1 * sin + x2 * cos], -1)

            q_pe = rope_rows(qfull[..., MLA_NOPE:])
            q_lat = jnp.einsum('thn,chn->thc', qfull[..., :MLA_NOPE], w_uk)
            c = _rmsnorm_jax(kva[:, :MLA_KV_LORA], g_d_kvnorm)
            k_pe = rope_rows(kva[:, None, MLA_KV_LORA:])[:, 0, :]
            qcat = jnp.concatenate([q_lat, q_pe], -1)
            lat = jnp.concatenate([c, k_pe], -1)
            qp, qs = split(qcat)
            latp, lats = split(lat)
            op = lax.map(lambda a: _mla_seq(a[0], a[1], pos_p), (qp, latp))

            def sample_seq(a):
                qq, lat_new, pt = a
                lat_all = jnp.concatenate([_gather_pages(cache_d_latent, pt), lat_new], axis=0)
                return _mla_seq(qq, lat_all, pos_s)
            os_ = lax.map(sample_seq, (qs, lats, page_table))
            oc = jnp.concatenate([op.reshape(NP, N_HEADS, MLA_KV_LORA), os_.reshape(NS, N_HEADS, MLA_KV_LORA)], 0)
            o = jnp.einsum('thc,chv->thv', oc, w_uv).reshape(N, N_HEADS * MLA_V)
            x = _matmul(o, w_d_o, tm=TM, tn=512, res=x)
            outs['d'] = (latp, lats)

        h2 = _rmsnorm(x, g_ffn_norm[layer], BF16, TM)
        act = _ffn_in(h2, w_ffn_in, layer, tm=TM, tf=512)
        x = _matmul(act, w_ffn_out_bf, tm=TM, tn=512, res=x, w_layer=layer)

    y = _rmsnorm(x, g_final_norm, F32, TM)
    y_prompt = y[:NP].reshape(B, S, D)
    y_sample = y[NP:].reshape(DB, T, D)
    a_kv_prompt, a_kv_sample = outs['a_kv']
    new_p, new_s = outs['b']
    c_kv_prompt, c_kv_sample, c_win_prompt, c_win_sample = outs['c']
    d_latent_prompt, d_latent_sample = outs['d']
    return (y_prompt, y_sample, a_kv_prompt, a_kv_sample, new_p[0], new_s[0], new_p[1], new_s[1],
            new_p[2], new_s[2], c_kv_prompt, c_kv_sample, c_win_prompt, c_win_sample,
            d_latent_prompt, d_latent_sample)
```

```python
import functools
import math

import jax
import jax.numpy as jnp
import numpy as np
from jax import lax
from jax.experimental import pallas as pl
from jax.experimental.pallas import tpu as pltpu

F32 = jnp.float32
BF16 = jnp.bfloat16

D_MODEL = 2048
DEPTH = 4
PAST_LEN = 8192
HEAD_DIM = 128
N_HEADS = 16
D_ATTN = N_HEADS * HEAD_DIM
ATTN_SCALE = HEAD_DIM ** -0.5
D_FF = 5632
RMS_EPS = 1e-6
REL_BUCKETS = 32
REL_MAX_DIST = 128
NEG_INF = -1e30
FORCE_SCORE = 1e30
GATHER_Q_BLOCK = 16
DENSE_Q_BLOCK = 128
MOBA_BLOCK = 256
MOBA_TOPK = 3
MOBA_KV_HEADS = 2
DIL_PATTERNS = ((128, 1), (512, 4), (2048, 16))
DIL_KV_HEADS = 4
CMP_BLOCK = 32
CMP_STRIDE = 16
SEL_BLOCK = 64
SEL_TOPK = 16
WIN = 512
WIN_BLOCK = 128
NSA_KV_ROWS = 6
MLA_Q_LORA = 512
MLA_KV_LORA = 512
MLA_NOPE = 128
MLA_ROPE = 64
MLA_V = 128
MLA_SCALE = (MLA_NOPE + MLA_ROPE) ** -0.5
ROPE_THETA = 10000.0

VMEM_LIMIT_BYTES = 56 * 1024 * 1024
LANE = 128


def _compiler_params(n_grid_axes):
    return pltpu.CompilerParams(
        dimension_semantics=("arbitrary",) * n_grid_axes,
        vmem_limit_bytes=VMEM_LIMIT_BYTES,
    )


def _rmsnorm_kernel(x_ref, g_ref, o_ref):
    x = x_ref[...]
    ms = jnp.mean(x * x, axis=-1, keepdims=True)
    o_ref[...] = ((x * lax.rsqrt(ms + RMS_EPS)) * g_ref[...]).astype(o_ref.dtype)


def _rmsnorm(x, g, out_dtype, tm):
    n, d = x.shape
    return pl.pallas_call(
        _rmsnorm_kernel,
        grid=(n // tm,),
        in_specs=[pl.BlockSpec((tm, d), lambda i: (i, 0)), pl.BlockSpec((1, d), lambda i: (0, 0))],
        out_specs=pl.BlockSpec((tm, d), lambda i: (i, 0)),
        out_shape=jax.ShapeDtypeStruct((n, d), out_dtype),
        compiler_params=_compiler_params(1),
        name="rmsnorm",
    )(x, g.reshape(1, d))


def _mm_kernel(*refs, has_res, cast_w, bf_copy):
    x_ref, w_ref = refs[0], refs[1]
    rest = refs[2:]
    if has_res:
        r_ref, rest = rest[0], rest[1:]
    o_ref = rest[0]
    rest = rest[1:]
    if bf_copy:
        ob_ref, rest = rest[0], rest[1:]
    if cast_w:
        wb_ref = rest[0]

        @pl.when(pl.program_id(1) == 0)
        def _():
            wb_ref[...] = w_ref[...].astype(BF16)

        w = wb_ref[...]
    else:
        w = w_ref[...]
    acc = jnp.dot(x_ref[...].astype(BF16), w, preferred_element_type=F32)
    if has_res:
        acc = r_ref[...] + acc
    o_ref[...] = acc.astype(o_ref.dtype)
    if bf_copy:
        ob_ref[...] = acc.astype(BF16)


def _matmul(x, w, *, tm, tn, res=None, out_dtype=F32, w_layer=None, w_col0=0, n_out=None, bf_copy=False):
    m, k = x.shape
    n = n_out if n_out is not None else w.shape[-1]
    assert m % tm == 0 and n % tn == 0 and w_col0 % tn == 0
    cast_w = w.dtype != BF16
    cb = w_col0 // tn
    if w_layer is None:
        w_spec = pl.BlockSpec((k, tn), lambda j, i: (0, j + cb))
    else:
        w_spec = pl.BlockSpec((None, k, tn), lambda j, i: (w_layer, 0, j + cb))
    in_specs = [pl.BlockSpec((tm, k), lambda j, i: (i, 0)), w_spec]
    args = [x, w]
    if res is not None:
        in_specs.append(pl.BlockSpec((tm, tn), lambda j, i: (i, j)))
        args.append(res)
    out_spec = pl.BlockSpec((tm, tn), lambda j, i: (i, j))
    out_shape = jax.ShapeDtypeStruct((m, n), out_dtype)
    return pl.pallas_call(
        functools.partial(_mm_kernel, has_res=res is not None, cast_w=cast_w, bf_copy=bf_copy),
        grid=(n // tn, m // tm),
        in_specs=in_specs,
        out_specs=[out_spec, out_spec] if bf_copy else out_spec,
        out_shape=[out_shape, jax.ShapeDtypeStruct((m, n), BF16)] if bf_copy else out_shape,
        scratch_shapes=[pltpu.VMEM((k, tn), BF16)] if cast_w else [],
        compiler_params=_compiler_params(2),
        name="matmul",
    )(*args)


def _ffn_in_kernel(x_ref, wg_ref, wu_ref, o_ref, wgb_ref, wub_ref):
    @pl.when(pl.program_id(1) == 0)
    def _():
        wgb_ref[...] = wg_ref[...].astype(BF16)
        wub_ref[...] = wu_ref[...].astype(BF16)

    x = x_ref[...]
    g = jnp.dot(x, wgb_ref[...], preferred_element_type=F32)
    u = jnp.dot(x, wub_ref[...], preferred_element_type=F32)
    o_ref[...] = ((g * jax.nn.sigmoid(g)) * u).astype(o_ref.dtype)


def _ffn_in(h, w_ffn_in, layer, *, tm, tf):
    m, k = h.shape
    nf = D_FF // tf
    return pl.pallas_call(
        _ffn_in_kernel,
        grid=(nf, m // tm),
        in_specs=[
            pl.BlockSpec((tm, k), lambda j, i: (i, 0)),
            pl.BlockSpec((None, k, tf), lambda j, i: (layer, 0, j)),
            pl.BlockSpec((None, k, tf), lambda j, i: (layer, 0, j + nf)),
        ],
        out_specs=pl.BlockSpec((tm, tf), lambda j, i: (i, j)),
        out_shape=jax.ShapeDtypeStruct((m, D_FF), BF16),
        scratch_shapes=[pltpu.VMEM((k, tf), BF16), pltpu.VMEM((k, tf), BF16)],
        compiler_params=_compiler_params(2),
        name="ffn_in",
    )(h, w_ffn_in, w_ffn_in)


def _np_bucket(dist):
    n = np.maximum(dist, 0)
    max_exact = REL_BUCKETS // 2
    large = max_exact + (np.log(np.maximum(n, 1).astype(np.float32) / np.float32(max_exact))
                         / np.float32(math.log(REL_MAX_DIST / max_exact))
                         * np.float32(REL_BUCKETS - max_exact)).astype(np.int32)
    large = np.minimum(large, REL_BUCKETS - 1)
    return np.where(n < max_exact, n, large).astype(np.int32)


def _bias_table(rel_bias, dist):
    b = rel_bias[jnp.asarray(_np_bucket(dist))]
    return jnp.moveaxis(b, -1, 0)


_NT = (((1,), (1,)), ((), ()))


def _moba_prompt_kernel(q_ref, kbf_ref, vbf_ref, kf_ref, d0_ref, d1_ref, bfar_ref, o_ref, kmean_ref, *, nb, group):
    h = pl.program_id(1)
    qi = pl.program_id(2)
    blk = MOBA_BLOCK

    @pl.when((qi == 0) & (h % group == 0))
    def _():
        kf = kf_ref[...]
        kmean_ref[...] = jnp.mean(kf.reshape(nb, blk, HEAD_DIM), axis=1).astype(BF16)

    q = q_ref[...]
    gate = lax.dot_general(q, kmean_ref[...], _NT, preferred_element_type=F32)
    col = lax.broadcasted_iota(jnp.int32, gate.shape, 1)
    colf = col.astype(F32)
    g = jnp.where(col < qi, gate, NEG_INF)
    sel = jnp.zeros(gate.shape, F32)
    for _ in range(MOBA_TOPK):
        mx = jnp.max(g, axis=-1, keepdims=True)
        idx = jnp.min(jnp.where(g == mx, colf, float(nb)), axis=-1, keepdims=True)
        pick = colf == idx
        sel = jnp.where(pick, 1.0, sel)
        g = jnp.where(pick, -jnp.inf, g)
    sel = jnp.where(col < qi, sel, 0.0)

    def logits(n):
        start = pl.multiple_of(n * blk, blk)
        k = kbf_ref[pl.ds(start, blk), :]
        return lax.dot_general(q, k, _NT, preferred_element_type=F32) * ATTN_SCALE

    def values(n):
        start = pl.multiple_of(n * blk, blk)
        return vbf_ref[pl.ds(start, blk), :]

    row = lax.broadcasted_iota(jnp.int32, (blk, blk), 0)
    colk = lax.broadcasted_iota(jnp.int32, (blk, blk), 1)
    s = jnp.where(row >= colk, logits(qi) + d0_ref[...], NEG_INF)
    m = jnp.max(s, axis=-1, keepdims=True)
    p = jnp.exp(s - m)
    l = jnp.sum(p, axis=-1, keepdims=True)
    acc = jnp.dot(p.astype(BF16), values(qi), preferred_element_type=F32)

    def body(n, carry):
        m, l, acc = carry
        chosen = jnp.sum(jnp.where(col == n, sel, 0.0), axis=-1, keepdims=True) > 0.5
        bias = jnp.where(n == qi - 1, d1_ref[...], bfar_ref[0:1, :])
        s = jnp.where(chosen, logits(n) + bias, NEG_INF)
        m_new = jnp.maximum(m, jnp.max(s, axis=-1, keepdims=True))
        alpha = jnp.exp(m - m_new)
        p = jnp.exp(s - m_new)
        l = alpha * l + jnp.sum(p, axis=-1, keepdims=True)
        acc = alpha * acc + jnp.dot(p.astype(BF16), values(n), preferred_element_type=F32)
        return m_new, l, acc

    m, l, acc = lax.fori_loop(0, qi, body, (m, l, acc))
    o_ref[...] = (acc / l).astype(o_ref.dtype)


def _moba_prompt(qkv, qkv_bf, rel_bias, B, S):
    blk = MOBA_BLOCK
    nb = S // blk
    group = N_HEADS // MOBA_KV_HEADS
    qcols = D_ATTN // HEAD_DIM
    i = np.arange(blk)
    d0 = _bias_table(rel_bias, i[:, None] - i[None, :])
    d1 = _bias_table(rel_bias, blk + i[:, None] - i[None, :])
    bfar = jnp.broadcast_to(rel_bias[_np_bucket(np.array(2 * blk))][:, None, None], (N_HEADS, 8, blk))
    return pl.pallas_call(
        functools.partial(_moba_prompt_kernel, nb=nb, group=group),
        grid=(B, N_HEADS, nb),
        in_specs=[
            pl.BlockSpec((blk, HEAD_DIM), lambda b, h, i: (b * nb + i, h)),
            pl.BlockSpec((S, HEAD_DIM), lambda b, h, i: (b, qcols + h // group)),
            pl.BlockSpec((S, HEAD_DIM), lambda b, h, i: (b, qcols + MOBA_KV_HEADS + h // group)),
            pl.BlockSpec((S, HEAD_DIM), lambda b, h, i: (b, qcols + h // group)),
            pl.BlockSpec((None, blk, blk), lambda b, h, i: (h, 0, 0)),
            pl.BlockSpec((None, blk, blk), lambda b, h, i: (h, 0, 0)),
            pl.BlockSpec((None, 8, blk), lambda b, h, i: (h, 0, 0)),
        ],
        out_specs=pl.BlockSpec((blk, HEAD_DIM), lambda b, h, i: (b * nb + i, h)),
        out_shape=jax.ShapeDtypeStruct((B * S, D_ATTN), BF16),
        scratch_shapes=[pltpu.VMEM((nb, HEAD_DIM), BF16)],
        compiler_params=_compiler_params(3),
        name="moba_prompt",
    )(qkv_bf, qkv_bf, qkv_bf, qkv, d0, d1, bfar)


PAGES_PER_STEP = 8


def _moba_sample_kernel(pt_ref, x_ref, bias_ref, b0_ref, *rest, group):
    pages = rest[:PAGES_PER_STEP]
    o_ref, g_scr, m_scr, l_scr, o_scr = rest[PAGES_PER_STEP:]
    j = pl.program_id(1)
    nsteps = pl.num_programs(1)
    pages_per_blk = MOBA_BLOCK // pages[0].shape[0]
    blks_per_step = PAGES_PER_STEP // pages_per_blk
    x = x_ref[...]
    q_bf = x[:N_HEADS].astype(BF16)
    qf = q_bf.astype(F32)
    lanes = (group, HEAD_DIM)
    for kvh in range(MOBA_KV_HEADS):
        rows = slice(kvh * group, (kvh + 1) * group)
        kcols = slice(kvh * HEAD_DIM, (kvh + 1) * HEAD_DIM)
        vcols = slice((MOBA_KV_HEADS + kvh) * HEAD_DIM, (MOBA_KV_HEADS + kvh + 1) * HEAD_DIM)
        for bi in range(blks_per_step):
            n = j * blks_per_step + bi
            prs = pages[bi * pages_per_blk:(bi + 1) * pages_per_blk]
            k = jnp.concatenate([p[:, kcols] for p in prs], axis=0)
            v = jnp.concatenate([p[:, vcols] for p in prs], axis=0)
            kmean = jnp.mean(k, axis=0, keepdims=True).astype(BF16).astype(F32)
            gate = jnp.sum(qf[rows] * kmean, axis=-1, keepdims=True)
            s = (lax.dot_general(q_bf[rows], k.astype(BF16), _NT, preferred_element_type=F32) * ATTN_SCALE
                 + bias_ref[n, rows, :])
            m = jnp.max(s, axis=-1, keepdims=True)
            p = jnp.exp(s - m)
            l = jnp.sum(p, axis=-1, keepdims=True)
            o = jnp.dot(p.astype(BF16), v.astype(BF16), preferred_element_type=F32)
            g_scr[n, rows, :] = jnp.broadcast_to(gate, lanes)
            m_scr[n, rows, :] = jnp.broadcast_to(m, lanes)
            l_scr[n, rows, :] = jnp.broadcast_to(l, lanes)
            o_scr[n, rows, :] = o

    @pl.when(j == nsteps - 1)
    def _():
        g = g_scr[...]
        nidx = lax.broadcasted_iota(jnp.int32, g.shape, 0).astype(F32)
        sel = jnp.zeros(g.shape, F32)
        for _ in range(MOBA_TOPK):
            mx = jnp.max(g, axis=0, keepdims=True)
            idx = jnp.min(jnp.where(g == mx, nidx, float(g.shape[0])), axis=0, keepdims=True)
            pick = nidx == idx
            sel = jnp.where(pick, 1.0, sel)
            g = jnp.where(pick, -jnp.inf, g)
        chosen = sel > 0.5
        k_new = x[N_HEADS:N_HEADS + MOBA_KV_HEADS].astype(BF16).astype(F32)
        v_new = x[N_HEADS + MOBA_KV_HEADS:N_HEADS + 2 * MOBA_KV_HEADS].astype(BF16).astype(F32)
        k_rows = jnp.concatenate([jnp.broadcast_to(k_new[i:i + 1], lanes) for i in range(MOBA_KV_HEADS)], axis=0)
        v_rows = jnp.concatenate([jnp.broadcast_to(v_new[i:i + 1], lanes) for i in range(MOBA_KV_HEADS)], axis=0)
        s_new = jnp.sum(qf * k_rows, axis=-1, keepdims=True) * ATTN_SCALE + b0_ref[...]
        m_blk = m_scr[...]
        m_tot = jnp.maximum(jnp.max(jnp.where(chosen, m_blk, NEG_INF), axis=0), s_new)
        w = jnp.where(chosen, jnp.exp(m_blk - m_tot[None]), 0.0)
        e_new = jnp.exp(s_new - m_tot)
        l_tot = jnp.sum(w * l_scr[...], axis=0) + e_new
        o_ref[...] = (jnp.sum(w * o_scr[...], axis=0) + e_new * v_rows) / l_tot


def _moba_sample(x_s, cache_a_kv, page_table, rel_bias):
    DB, n_pages = page_table.shape
    n_pool, page = cache_a_kv.shape[:2]
    group = N_HEADS // MOBA_KV_HEADS
    nblk = n_pages * page // MOBA_BLOCK
    width = 2 * MOBA_KV_HEADS * HEAD_DIM
    pool = cache_a_kv.reshape(n_pool, page, width)
    kpos = np.arange(nblk * MOBA_BLOCK).reshape(nblk, MOBA_BLOCK)
    bias = jnp.transpose(_bias_table(rel_bias, PAST_LEN - kpos), (1, 0, 2))
    b0 = jnp.broadcast_to(rel_bias[0][:, None], (N_HEADS, HEAD_DIM))
    nsteps = n_pages // PAGES_PER_STEP
    rows = x_s.shape[1]

    def page_spec(i):
        return pl.BlockSpec((None, page, width), lambda b, j, pt: (pt[b, j * PAGES_PER_STEP + i], 0, 0))

    grid_spec = pltpu.PrefetchScalarGridSpec(
        num_scalar_prefetch=1,
        grid=(DB, nsteps),
        in_specs=[
            pl.BlockSpec((None, rows, HEAD_DIM), lambda b, j, pt: (b, 0, 0)),
            pl.BlockSpec((nblk, N_HEADS, MOBA_BLOCK), lambda b, j, pt: (0, 0, 0)),
            pl.BlockSpec((N_HEADS, HEAD_DIM), lambda b, j, pt: (0, 0)),
        ] + [page_spec(i) for i in range(PAGES_PER_STEP)],
        out_specs=pl.BlockSpec((None, N_HEADS, HEAD_DIM), lambda b, j, pt: (b, 0, 0)),
        scratch_shapes=[pltpu.VMEM((nblk, N_HEADS, HEAD_DIM), F32) for _ in range(4)],
    )
    return pl.pallas_call(
        functools.partial(_moba_sample_kernel, group=group),
        grid_spec=grid_spec,
        out_shape=jax.ShapeDtypeStruct((DB, N_HEADS, HEAD_DIM), F32),
        compiler_params=_compiler_params(2),
        name="moba_sample",
    )(page_table, x_s, bias, b0, *([pool] * PAGES_PER_STEP))


def _bucket_thresholds():
    table = _np_bucket(np.arange(2 * REL_MAX_DIST))
    return [int(np.argmax(table >= b)) for b in range(REL_BUCKETS)]


def _bias_from_dist(dist, tab_ref, h):
    thr = _bucket_thresholds()
    bias = jnp.full(dist.shape, tab_ref[h, 0], F32)
    for b in range(1, REL_BUCKETS):
        bias = jnp.where(dist >= thr[b], tab_ref[h, b], bias)
    return bias


def _compress_kernel(a_ref, pe_ref, w1_ref, w2_ref, o_ref):
    half = CMP_STRIDE * HEAD_DIM
    a = a_ref[...]
    a_lo = (a + pe_ref[:, :half]).astype(BF16)
    a_hi = (a + pe_ref[:, half:]).astype(BF16)
    w1 = w1_ref[...].astype(BF16)
    p_lo = jnp.dot(a_lo, w1[:half], preferred_element_type=F32)
    p_hi = jnp.dot(a_hi, w1[half:], preferred_element_type=F32)
    nch = a.shape[0]
    hid = jax.nn.gelu(p_lo + pltpu.roll(p_hi, nch - 1, 0))
    o_ref[...] = jnp.dot(hid.astype(BF16), w2_ref[...].astype(BF16), preferred_element_type=F32).astype(o_ref.dtype)


def _compress_call(chunks, w1, w2, pe):
    G, _, nch, width = chunks.shape
    pe_flat = pe.reshape(2, 1, CMP_BLOCK * HEAD_DIM)
    return pl.pallas_call(
        _compress_kernel,
        grid=(2, G),
        in_specs=[
            pl.BlockSpec((None, None, nch, width), lambda c, g: (g, c, 0, 0)),
            pl.BlockSpec((None, 1, CMP_BLOCK * HEAD_DIM), lambda c, g: (c, 0, 0)),
            pl.BlockSpec((None, CMP_BLOCK * HEAD_DIM, w1.shape[-1]), lambda c, g: (c, 0, 0)),
            pl.BlockSpec((None, w2.shape[1], HEAD_DIM), lambda c, g: (c, 0, 0)),
        ],
        out_specs=pl.BlockSpec((None, None, nch, HEAD_DIM), lambda c, g: (g, c, 0, 0)),
        out_shape=jax.ShapeDtypeStruct((G, 2, nch, HEAD_DIM), BF16),
        compiler_params=_compiler_params(2),
        name="nsa_compress",
    )(chunks, pe_flat, w1, w2)


def _nsa_cmp_kernel(tab_ref, q_ref, gl_ref, kc_ref, vc_ref, ov_ref, o_ref, sel_ref, *, pos0, tq):
    qt = pl.program_id(1)
    ncmp = kc_ref.shape[0]
    nsel = ov_ref.shape[1]
    pos = pos0 + qt * tq + lax.broadcasted_iota(jnp.int32, (tq, 1), 0)
    ncol = lax.broadcasted_iota(jnp.int32, (tq, ncmp), 1)
    dc = pos - (CMP_STRIDE * ncol + CMP_BLOCK - 1)
    valid = dc >= 0
    kc = kc_ref[...]
    vc = vc_ref[...]
    gates = jax.nn.sigmoid(gl_ref[...])
    pcsum = jnp.zeros((tq, ncmp), F32)
    for h in range(N_HEADS):
        qh = q_ref[:, h * HEAD_DIM:(h + 1) * HEAD_DIM]
        lc = lax.dot_general(qh, kc, _NT, preferred_element_type=F32) * ATTN_SCALE + _bias_from_dist(dc, tab_ref, h)
        lc = jnp.where(valid, lc, NEG_INF)
        m = jnp.max(lc, axis=-1, keepdims=True)
        e = jnp.where(valid, jnp.exp(lc - m), 0.0)
        s = jnp.maximum(jnp.sum(e, axis=-1, keepdims=True), 1e-30)
        pc = e / s
        pcsum = pcsum + pc
        o_cmp = jnp.dot(pc.astype(BF16), vc, preferred_element_type=F32)
        o_ref[:, h * HEAD_DIM:(h + 1) * HEAD_DIM] = gates[:, 3 * h:3 * h + 1] * o_cmp
    imp = jnp.dot(pcsum.astype(BF16), ov_ref[...], preferred_element_type=F32)
    j = lax.broadcasted_iota(jnp.int32, (tq, nsel), 1)
    cur = jnp.right_shift(pos, int(math.log2(SEL_BLOCK)))
    forced = (j == 0) | (j == cur) | (j == cur - 1)
    score = jnp.where(j > cur, NEG_INF, jnp.where(forced, FORCE_SCORE, imp))
    rank = jnp.zeros((tq, nsel), F32)
    for jp in range(nsel):
        sj = score[:, jp:jp + 1]
        ahead = (sj > score) | ((sj == score) & (j > jp))
        rank = rank + jnp.where(ahead, 1.0, 0.0)
    sel_ref[...] = jnp.where(rank < float(min(SEL_TOPK, nsel)), 1.0, 0.0).astype(sel_ref.dtype)


def _flash_update(qh, k, v, bias, valid, m, l, acc):
    s = lax.dot_general(qh, k, _NT, preferred_element_type=F32) * ATTN_SCALE + bias
    s = jnp.where(valid, s, NEG_INF)
    m_new = jnp.maximum(m, jnp.max(s, axis=-1, keepdims=True))
    alpha = jnp.exp(m - m_new)
    p = jnp.exp(s - m_new)
    l = alpha * l + jnp.sum(p, axis=-1, keepdims=True)
    acc = alpha * acc + jnp.dot(p.astype(BF16), v, preferred_element_type=F32)
    return m_new, l, acc


def _nsa_selwin_kernel(q_ref, ks_ref, vs_ref, kw_ref, vw_ref, selm_ref, e_ref, gl_ref, ocmp_ref, d0_ref, d1_ref,
                       bfar_ref, o_ref, kmask_ref, *, tq):
    qt = pl.program_id(1)
    h = pl.program_id(2)

    @pl.when(h == 0)
    def _():
        kmask_ref[...] = jnp.dot(selm_ref[...], e_ref[...], preferred_element_type=F32)

    qh = q_ref[...]
    row = lax.broadcasted_iota(jnp.int32, (tq, tq), 0)
    colk = lax.broadcasted_iota(jnp.int32, (tq, tq), 1)
    causal = row >= colk
    d0 = d0_ref[...]
    bfar = bfar_ref[0:1, :]

    def tile(ref, t):
        return ref[pl.ds(pl.multiple_of(t * tq, tq), tq), :]

    def kmask(t):
        return kmask_ref[:, pl.ds(pl.multiple_of(t * tq, tq), tq)] > 0.5

    def first(k, v, valid):
        s = lax.dot_general(qh, k, _NT, preferred_element_type=F32) * ATTN_SCALE + d0
        s = jnp.where(valid, s, NEG_INF)
        m = jnp.max(s, axis=-1, keepdims=True)
        p = jnp.exp(s - m)
        return m, jnp.sum(p, axis=-1, keepdims=True), jnp.dot(p.astype(BF16), v, preferred_element_type=F32)

    state = first(tile(ks_ref, qt), tile(vs_ref, qt), causal & kmask(qt))

    def sel_body(t, st):
        bias = jnp.where(t == qt - 1, d1_ref[...], bfar)
        return _flash_update(qh, tile(ks_ref, t), tile(vs_ref, t), bias, kmask(t), *st)

    m, l, acc = lax.fori_loop(0, qt, sel_body, state)
    o_sel = acc / l

    m, l, acc = first(tile(kw_ref, qt), tile(vw_ref, qt), causal)
    n_back = WIN // tq
    for delta in range(1, n_back + 1):
        t = jnp.maximum(qt - delta, 0)
        inside = jnp.full((tq, tq), qt, jnp.int32) >= delta
        if delta == 1:
            bias, valid = d1_ref[...], inside
        elif delta < n_back:
            bias, valid = bfar, inside
        else:
            bias, valid = bfar, (row <= colk) & inside
        m, l, acc = _flash_update(qh, tile(kw_ref, t), tile(vw_ref, t), bias, valid, m, l, acc)
    o_win = acc / l

    gates = jax.nn.sigmoid(gl_ref[...])
    gcol = lax.broadcasted_iota(jnp.int32, gates.shape, 1)
    g_sel = jnp.sum(jnp.where(gcol == 3 * h + 1, gates, 0.0), axis=-1, keepdims=True)
    g_win = jnp.sum(jnp.where(gcol == 3 * h + 2, gates, 0.0), axis=-1, keepdims=True)
    o_ref[...] = ((ocmp_ref[...] + g_sel * o_sel) + g_win * o_win).astype(o_ref.dtype)


def _nsa_prompt(qkv, qkv_bf, gl, rel_bias, w_cmp1, w_cmp2, w_cmp_pe, B, S):
    tq = WIN_BLOCK
    nqt = S // tq
    nch = S // CMP_STRIDE
    nsel = S // SEL_BLOCK
    qcols = D_ATTN // HEAD_DIM
    kv_cmp = qkv[:B * S, D_ATTN:D_ATTN + 2 * HEAD_DIM].reshape(B, S, 2, HEAD_DIM)
    chunks = kv_cmp.transpose(0, 2, 1, 3).reshape(B, 2, nch, CMP_STRIDE * HEAD_DIM)
    cmp = _compress_call(chunks, w_cmp1, w_cmp2, w_cmp_pe)
    n = np.arange(nch)[:, None]
    jj = np.arange(nsel)[None, :]
    overlap = ((n * CMP_STRIDE < jj * SEL_BLOCK + SEL_BLOCK) & (n * CMP_STRIDE + CMP_BLOCK > jj * SEL_BLOCK)
               & (n < nch - 1))
    ov = jnp.asarray(overlap, BF16)
    expand = jnp.asarray(np.arange(S)[None, :] // SEL_BLOCK == np.arange(nsel)[:, None], BF16)
    tab = rel_bias.T
    ocmp, selm = pl.pallas_call(
        functools.partial(_nsa_cmp_kernel, pos0=0, tq=tq),
        grid=(B, nqt),
        in_specs=[
            pl.BlockSpec(memory_space=pltpu.SMEM),
            pl.BlockSpec((tq, D_ATTN), lambda b, t: (b * nqt + t, 0)),
            pl.BlockSpec((tq, 3 * N_HEADS), lambda b, t: (b * nqt + t, 0)),
            pl.BlockSpec((None, None, nch, HEAD_DIM), lambda b, t: (b, 0, 0, 0)),
            pl.BlockSpec((None, None, nch, HEAD_DIM), lambda b, t: (b, 1, 0, 0)),
            pl.BlockSpec((nch, nsel), lambda b, t: (0, 0)),
        ],
        out_specs=[
            pl.BlockSpec((tq, D_ATTN), lambda b, t: (b * nqt + t, 0)),
            pl.BlockSpec((tq, nsel), lambda b, t: (b * nqt + t, 0)),
        ],
        out_shape=[jax.ShapeDtypeStruct((B * S, D_ATTN), F32), jax.ShapeDtypeStruct((B * S, nsel), BF16)],
        compiler_params=_compiler_params(2),
        name="nsa_cmp",
    )(tab, qkv_bf, gl, cmp, cmp, ov)

    i = np.arange(tq)
    d0 = _bias_table(rel_bias, i[:, None] - i[None, :])
    d1 = _bias_table(rel_bias, tq + i[:, None] - i[None, :])
    bfar = jnp.broadcast_to(rel_bias[_np_bucket(np.array(2 * tq))][:, None, None], (N_HEADS, 8, tq))
    seq_spec = lambda c: pl.BlockSpec((S, HEAD_DIM), lambda b, t, h: (b, qcols + c))
    tile_spec = lambda w: pl.BlockSpec((tq, w), lambda b, t, h: (b * nqt + t, 0))
    head_tile = pl.BlockSpec((tq, HEAD_DIM), lambda b, t, h: (b * nqt + t, h))
    return pl.pallas_call(
        functools.partial(_nsa_selwin_kernel, tq=tq),
        grid=(B, nqt, N_HEADS),
        in_specs=[
            head_tile, seq_spec(2), seq_spec(3), seq_spec(4), seq_spec(5),
            tile_spec(nsel),
            pl.BlockSpec((nsel, S), lambda b, t, h: (0, 0)),
            tile_spec(3 * N_HEADS),
            head_tile,
            pl.BlockSpec((None, tq, tq), lambda b, t, h: (h, 0, 0)),
            pl.BlockSpec((None, tq, tq), lambda b, t, h: (h, 0, 0)),
            pl.BlockSpec((None, 8, tq), lambda b, t, h: (h, 0, 0)),
        ],
        out_specs=head_tile,
        out_shape=jax.ShapeDtypeStruct((B * S, D_ATTN), BF16),
        scratch_shapes=[pltpu.VMEM((tq, S), F32)],
        compiler_params=_compiler_params(3),
        name="nsa_selwin",
    )(qkv_bf, qkv_bf, qkv_bf, qkv_bf, qkv_bf, selm, expand, gl, ocmp, d0, d1, bfar)


def _rel_bucket(dist):
    n = jnp.maximum(dist, 0)
    max_exact = REL_BUCKETS // 2
    large = max_exact + (jnp.log(jnp.maximum(n, 1).astype(F32) / max_exact)
                         / math.log(REL_MAX_DIST / max_exact) * (REL_BUCKETS - max_exact)).astype(jnp.int32)
    large = jnp.minimum(large, REL_BUCKETS - 1)
    return jnp.where(n < max_exact, n, large)


def _softmax_lse(logits, mask, axis):
    l = jnp.where(mask, logits.astype(F32), NEG_INF)
    m = jnp.max(l, axis=axis, keepdims=True)
    e = jnp.where(mask, jnp.exp(l - m), 0.0)
    s = jnp.sum(e, axis=axis, keepdims=True)
    s = jnp.maximum(s, 1e-30)
    return e / s, jnp.squeeze(m + jnp.log(s), axis)


def _rope(x, pos):
    half = x.shape[-1] // 2
    inv = ROPE_THETA ** (-jnp.arange(half, dtype=F32) / half)
    ang = pos.astype(F32)[:, None] * inv[None, :]
    cos = jnp.cos(ang)[:, None, :]
    sin = jnp.sin(ang)[:, None, :]
    xf = x.astype(F32)
    x1, x2 = xf[..., :half], xf[..., half:]
    return jnp.concatenate([x1 * cos - x2 * sin, x1 * sin + x2 * cos], -1).astype(x.dtype)


def _rmsnorm_jax(x, g):
    xf = x.astype(F32)
    y = xf * lax.rsqrt(jnp.mean(xf * xf, axis=-1, keepdims=True) + RMS_EPS)
    return (y * g.astype(F32)).astype(x.dtype)


def _gather_pages(pool, pt_row):
    g = pool[pt_row]
    return g.reshape((-1,) + pool.shape[2:])


def _map_query_blocks(fn, q_arrays, q_pos, block):
    T = q_pos.shape[0]
    if T <= block or T % block:
        return fn(*q_arrays, q_pos)
    nb = T // block
    xs = tuple(a.reshape((nb, block) + a.shape[1:]) for a in q_arrays) + (q_pos.reshape(nb, block),)
    out = lax.map(lambda a: fn(*a), xs)
    return out.reshape((T,) + out.shape[2:])


def _moba_seq(q, kv, q_pos, rel_bias):
    L = kv.shape[0]
    nb = -(-L // MOBA_BLOCK)
    kv = jnp.pad(kv, ((0, nb * MOBA_BLOCK - L), (0, 0), (0, 0), (0, 0)))
    kvb = kv.reshape(nb, MOBA_BLOCK, 2, MOBA_KV_HEADS, HEAD_DIM).transpose(2, 3, 0, 1, 4)
    kb, vb = kvb[0], kvb[1]
    kmean = jnp.mean(kb.astype(F32), axis=2)
    topk = min(MOBA_TOPK, nb)
    G = N_HEADS // MOBA_KV_HEADS
    tb = rel_bias.T.reshape(MOBA_KV_HEADS, G, REL_BUCKETS)
    kv_i = jnp.arange(MOBA_KV_HEADS)[None, :, None, None]
    g_i = jnp.arange(G)[None, None, :, None, None]

    def block_fn(qb, pos):
        Tb = qb.shape[0]
        qg = qb.reshape(Tb, MOBA_KV_HEADS, G, HEAD_DIM)
        own = pos // MOBA_BLOCK
        gate = jnp.einsum('tkgd,knd->tkgn', qg.astype(F32), kmean)
        past = jnp.arange(nb)[None, :] < own[:, None]
        gate = jnp.where(past[:, None, None, :], gate, NEG_INF)
        _, sel = lax.top_k(gate, topk)
        sel_ok = sel < own[:, None, None, None]
        own_b = jnp.broadcast_to(own[:, None, None, None], (Tb, MOBA_KV_HEADS, G, 1))
        blocks = jnp.concatenate([sel, own_b], -1)
        ok_blk = jnp.concatenate([sel_ok, jnp.ones_like(own_b, dtype=bool)], -1)
        kg = kb[kv_i, blocks]
        vg = vb[kv_i, blocks]
        kpos = blocks[..., None] * MOBA_BLOCK + jnp.arange(MOBA_BLOCK)
        dist = pos[:, None, None, None, None] - kpos
        logits = (jnp.einsum('tkgd,tkgnsd->tkgns', qg, kg).astype(F32) * ATTN_SCALE
                  + tb[kv_i[..., None], g_i, _rel_bucket(dist)])
        mask = ok_blk[..., None] & (dist >= 0)
        nk = (topk + 1) * MOBA_BLOCK
        p, _ = _softmax_lse(logits.reshape(Tb, MOBA_KV_HEADS, G, nk), mask.reshape(Tb, MOBA_KV_HEADS, G, nk), -1)
        out = jnp.einsum('tkgj,tkgjd->tkgd', p.astype(vg.dtype), vg.reshape(Tb, MOBA_KV_HEADS, G, nk, HEAD_DIM))
        return out.reshape(Tb, N_HEADS, HEAD_DIM)

    return _map_query_blocks(block_fn, (q,), q_pos, GATHER_Q_BLOCK)


def _moba_core(qp, kvp, qs, kvs, cache_a_kv, page_table, pos_p, pos_s, rel_bias):
    op = lax.map(lambda a: _moba_seq(a[0], a[1], pos_p, rel_bias), (qp, kvp))

    def sample_seq(a):
        q, kv_new, pt = a
        kv_all = jnp.concatenate([_gather_pages(cache_a_kv, pt), kv_new], axis=0)
        return _moba_seq(q, kv_all, pos_s, rel_bias)
    os_ = lax.map(sample_seq, (qs, kvs, page_table))
    return op, os_


def _dilated_prompt(q, kv, win, dil, rel_bias):
    B, S = q.shape[:2]
    R = win // dil
    Sp = -(-S // win) * win
    nbu = Sp // win
    G = N_HEADS // DIL_KV_HEADS
    q = jnp.pad(q, ((0, 0), (0, Sp - S), (0, 0), (0, 0)))
    kv = jnp.pad(kv, ((0, 0), (0, Sp - S), (0, 0), (0, 0), (0, 0)))
    qr = q.reshape(B, nbu, R, dil, DIL_KV_HEADS, G, HEAD_DIM)
    kvr = kv.reshape(B, nbu, R, dil, 2, DIL_KV_HEADS, HEAD_DIM)
    prev = jnp.pad(kvr, ((0, 0), (1, 0), (0, 0), (0, 0), (0, 0), (0, 0), (0, 0)))[:, :nbu]
    kk = jnp.concatenate([prev, kvr], axis=2)
    a = jnp.arange(R)[:, None]
    j = jnp.arange(2 * R)[None, :]
    steps = R + a - j
    band = (steps >= 0) & (steps <= R)
    kvalid = (jnp.arange(nbu)[:, None] > 0) | (jnp.arange(2 * R)[None, :] >= R)
    ok = band[None] & kvalid[:, None, :]
    bias = rel_bias[_rel_bucket(steps * dil)].transpose(2, 0, 1).reshape(DIL_KV_HEADS, G, R, 2 * R)
    logits = jnp.einsum('bnqrkgd,bnjrkd->bnrkgqj', qr, kk[:, :, :, :, 0]).astype(F32) * ATTN_SCALE + bias
    p, lse = _softmax_lse(logits, ok[None, :, None, None, None, :, :], -1)
    out = jnp.einsum('bnrkgqj,bnjrkd->bnqrkgd', p.astype(kk.dtype), kk[:, :, :, :, 1])
    out = out.reshape(B, Sp, N_HEADS, HEAD_DIM)[:, :S]
    lse = lse.transpose(0, 1, 5, 2, 3, 4).reshape(B, Sp, N_HEADS)[:, :S]
    return out, lse


def _dilated_sample(q, kv_new, buf, win, dil, rel_bias):
    DB, T = q.shape[:2]
    Lbuf = buf.shape[1]
    R = win // dil
    G = N_HEADS // DIL_KV_HEADS
    ext = jnp.concatenate([buf, kv_new], axis=1)
    m = jnp.arange(R + 1)
    idx = Lbuf + jnp.arange(T)[:, None] - m[None, :] * dil
    ok = idx >= 0
    g = ext[:, jnp.clip(idx, 0)]
    qg = q.reshape(DB, T, DIL_KV_HEADS, G, HEAD_DIM)
    bias = rel_bias[_rel_bucket(m * dil)].T.reshape(DIL_KV_HEADS, G, R + 1)
    logits = jnp.einsum('btkgd,btjkd->btkgj', qg, g[:, :, :, 0]).astype(F32) * ATTN_SCALE + bias
    p, lse = _softmax_lse(logits, ok[None, :, None, None, :], -1)
    out = jnp.einsum('btkgj,btjkd->btkgd', p.astype(g.dtype), g[:, :, :, 1])
    return out.reshape(DB, T, N_HEADS, HEAD_DIM), lse.reshape(DB, T, N_HEADS), ext[:, T:]


def _merge_groups(outs, lses):
    w = jax.nn.softmax(jnp.stack(lses, 0), axis=0)
    o = jnp.sum(w[..., None] * jnp.stack(outs, 0).astype(F32), axis=0)
    B, T = o.shape[:2]
    return o.reshape(B, T, D_ATTN).astype(outs[0].dtype)


def _dilated_core(qp, kvp, qs, kvs, bufs, rel_bias):
    S = qp.shape[1]
    outs_p, lses_p, outs_s, lses_s, new_p, new_s = [], [], [], [], [], []
    for gi, (win, dil) in enumerate(DIL_PATTERNS):
        o, l = _dilated_prompt(qp[:, :, gi], kvp[:, :, gi], win, dil, rel_bias)
        outs_p.append(o)
        lses_p.append(l)
        new_p.append(kvp[:, S - min(win, S):, gi])
        o, l, nbuf = _dilated_sample(qs[:, :, gi], kvs[:, :, gi], bufs[gi], win, dil, rel_bias)
        outs_s.append(o)
        lses_s.append(l)
        new_s.append(nbuf)
    return _merge_groups(outs_p, lses_p), _merge_groups(outs_s, lses_s), new_p, new_s


def _compress(kv2, w1, w2, pe):
    L = kv2.shape[0]
    n_cmp = (L - CMP_BLOCK) // CMP_STRIDE + 1
    idx = jnp.arange(n_cmp)[:, None] * CMP_STRIDE + jnp.arange(CMP_BLOCK)[None, :]
    blk = kv2[idx].transpose(2, 0, 1, 3) + pe[:, None]
    hid = jax.nn.gelu(jnp.einsum('cnx,cxh->cnh', blk.reshape(2, n_cmp, CMP_BLOCK * HEAD_DIM), w1))
    return jnp.einsum('cnh,chd->cnd', hid, w2)


def _nsa_seq(q, gates, kv4, q_pos, rel_bias, w_cmp1, w_cmp2, w_cmp_pe):
    L = kv4.shape[0]
    cmp = _compress(kv4[:, 0:2], w_cmp1, w_cmp2, w_cmp_pe)
    kc, vc = cmp[0], cmp[1]
    n_cmp = kc.shape[0]
    tok_start = jnp.arange(n_cmp) * CMP_STRIDE
    cmp_end = tok_start + CMP_BLOCK - 1
    n_sel = -(-L // SEL_BLOCK)
    sel_kv = jnp.pad(kv4[:, 2:4], ((0, n_sel * SEL_BLOCK - L), (0, 0), (0, 0))).reshape(n_sel, SEL_BLOCK, 2, HEAD_DIM)
    blk_start = jnp.arange(n_sel) * SEL_BLOCK
    overlap = ((tok_start[:, None] < blk_start[None, :] + SEL_BLOCK)
               & (tok_start[:, None] + CMP_BLOCK > blk_start[None, :])).astype(F32)
    n_top = min(SEL_TOPK, n_sel)

    def block_fn(qb, gb, pos):
        Tb = qb.shape[0]
        dc = pos[:, None] - cmp_end[None, :]
        lc = (jnp.einsum('thd,nd->thn', qb, kc).astype(F32) * ATTN_SCALE
              + rel_bias[_rel_bucket(dc)].transpose(0, 2, 1))
        pc, _ = _softmax_lse(lc, (dc >= 0)[:, None, :], -1)
        o_cmp = jnp.einsum('thn,nd->thd', pc.astype(vc.dtype), vc)
        imp = jnp.sum(pc, axis=1) @ overlap
        cur = pos // SEL_BLOCK
        j = jnp.arange(n_sel)[None, :]
        forced = (j == 0) | (j == cur[:, None]) | (j == cur[:, None] - 1)
        score = jnp.where(j > cur[:, None], NEG_INF, jnp.where(forced, FORCE_SCORE, imp))
        _, sidx = lax.top_k(score, n_top)
        g = sel_kv[sidx]
        kpos = sidx[..., None] * SEL_BLOCK + jnp.arange(SEL_BLOCK)
        ds = pos[:, None, None] - kpos
        ls = (jnp.einsum('thd,tnsd->thns', qb, g[..., 0, :]).astype(F32) * ATTN_SCALE
              + rel_bias[_rel_bucket(ds)].transpose(0, 3, 1, 2))
        nk = n_top * SEL_BLOCK
        ps, _ = _softmax_lse(ls.reshape(Tb, N_HEADS, nk), (ds >= 0).reshape(Tb, 1, nk), -1)
        o_sel = jnp.einsum('thj,tjd->thd', ps.astype(g.dtype), g[..., 1, :].reshape(Tb, nk, HEAD_DIM))
        return gb[..., 0:1] * o_cmp + gb[..., 1:2] * o_sel

    return _map_query_blocks(block_fn, (q, gates), q_pos, GATHER_Q_BLOCK)


def _window_prompt(q, kvw, rel_bias):
    B, S = q.shape[:2]
    P = WIN // WIN_BLOCK
    nb = S // WIN_BLOCK
    J = (P + 1) * WIN_BLOCK
    qr = q.reshape(B, nb, WIN_BLOCK, N_HEADS, HEAD_DIM)
    kvr = jnp.pad(kvw.reshape(B, nb, WIN_BLOCK, 2, HEAD_DIM), ((0, 0), (P, 0), (0, 0), (0, 0), (0, 0)))
    kk = jnp.concatenate([kvr[:, i:i + nb] for i in range(P + 1)], axis=2)
    dist = P * WIN_BLOCK + jnp.arange(WIN_BLOCK)[:, None] - jnp.arange(J)[None, :]
    kpos = jnp.arange(nb)[:, None] * WIN_BLOCK - P * WIN_BLOCK + jnp.arange(J)[None, :]
    ok = ((dist >= 0) & (dist <= WIN))[None] & (kpos >= 0)[:, None, :]
    bias = rel_bias[_rel_bucket(dist)].transpose(2, 0, 1)
    logits = jnp.einsum('bnqhd,bnjd->bnhqj', qr, kk[:, :, :, 0]).astype(F32) * ATTN_SCALE + bias
    p, _ = _softmax_lse(logits, ok[None, :, None], -1)
    out = jnp.einsum('bnhqj,bnjd->bnqhd', p.astype(kk.dtype), kk[:, :, :, 1])
    return out.reshape(B, S, N_HEADS, HEAD_DIM)


def _window_sample(q, kvw_new, buf, q_pos, rel_bias):
    T = q.shape[1]
    Lbuf = buf.shape[1]
    ext = jnp.concatenate([buf, kvw_new], axis=1)
    kpos = PAST_LEN - Lbuf + jnp.arange(Lbuf + T)
    dist = q_pos[:, None] - kpos[None, :]
    ok = (dist >= 0) & (dist <= WIN)
    bias = rel_bias[_rel_bucket(dist)].transpose(2, 0, 1)
    logits = jnp.einsum('bthd,bjd->bhtj', q, ext[:, :, 0]).astype(F32) * ATTN_SCALE + bias
    p, _ = _softmax_lse(logits, ok[None, None], -1)
    out = jnp.einsum('bhtj,bjd->bthd', p.astype(ext.dtype), ext[:, :, 1])
    return out, ext[:, T:]


def _nsa_sample_core(qs, kvs, gs, cache_c_kv, state_c_win, page_table, pos_s, rel_bias, w_cmp1, w_cmp2, w_cmp_pe):
    def sample_seq(a):
        q, g, kv_new, pt = a
        kv_all = jnp.concatenate([_gather_pages(cache_c_kv, pt), kv_new[:, :4]], axis=0)
        return _nsa_seq(q, g, kv_all, pos_s, rel_bias, w_cmp1, w_cmp2, w_cmp_pe)
    os_ = lax.map(sample_seq, (qs, gs, kvs, page_table))
    ow, win_s = _window_sample(qs, kvs[:, :, 4:6], state_c_win, pos_s, rel_bias)
    os_ = os_ + gs[..., 2:3] * ow
    return os_, win_s


def _mla_seq(q, lat, q_pos):
    kpos = jnp.arange(lat.shape[0])
    c = lat[:, :MLA_KV_LORA]

    def block_fn(qb, pos):
        logits = jnp.einsum('thc,lc->htl', qb, lat).astype(F32) * MLA_SCALE
        p, _ = _softmax_lse(logits, (kpos[None, :] <= pos[:, None])[None], -1)
        return jnp.einsum('htl,lc->thc', p.astype(c.dtype), c)

    return _map_query_blocks(block_fn, (q,), q_pos, DENSE_Q_BLOCK)


def kernel(x_prompt, x_sample, cache_a_kv, state_b_kv1, state_b_kv2, state_b_kv3, cache_c_kv, state_c_win,
           cache_d_latent, page_table, rel_bias, g_attn_norm, g_ffn_norm, g_final_norm,
           w_a_qkv, w_a_o, w_b_qkv, w_b_o, w_c_qkv, w_c_gate, w_c_cmp1, w_c_cmp2, w_c_cmp_pe, w_c_o,
           w_d_qa, g_d_qnorm, w_d_qb, w_d_kva, g_d_kvnorm, w_d_kvb, w_d_o, w_ffn_in, w_ffn_out):
    B, S, D = x_prompt.shape
    DB, T, _ = x_sample.shape
    NP = B * S
    NS = DB * T
    N = NP + NS
    TM = 640
    assert N % TM == 0
    pos_p = jnp.arange(S, dtype=jnp.int32)
    pos_s = PAST_LEN + jnp.arange(T, dtype=jnp.int32)

    x = jnp.concatenate([x_prompt.reshape(NP, D), x_sample.reshape(NS, D)], axis=0)
    w_ffn_out_bf = w_ffn_out.astype(BF16)

    def split(a):
        return a[:NP].reshape((B, S) + a.shape[1:]), a[NP:].reshape((DB, T) + a.shape[1:])

    outs = {}
    for layer in range(DEPTH):
        h = _rmsnorm(x, g_attn_norm[layer], BF16, TM)
        kind = layer % 4
        if kind == 0:
            qkv, qkv_bf = _matmul(h, w_a_qkv, tm=TM, tn=512, bf_copy=True)
            kv = qkv[:, D_ATTN:].reshape(N, 2, MOBA_KV_HEADS, HEAD_DIM)
            kvp, kvs = split(kv)
            op = _moba_prompt(qkv, qkv_bf, rel_bias, B, S)
            x_s = qkv[NP:].reshape(DB, N_HEADS + 2 * MOBA_KV_HEADS, HEAD_DIM)
            os_ = _moba_sample(x_s, cache_a_kv, page_table, rel_bias)
            o = jnp.concatenate([op, os_.reshape(NS, D_ATTN).astype(BF16)], axis=0)
            x = _matmul(o, w_a_o, tm=TM, tn=512, res=x)
            outs['a_kv'] = (kvp, kvs)
        elif kind == 1:
            n_g = len(DIL_PATTERNS)
            width = D_ATTN + 2 * DIL_KV_HEADS * HEAD_DIM
            qkv = _matmul(h, w_b_qkv, tm=TM, tn=512).reshape(N, n_g, width)
            q = qkv[..., :D_ATTN].reshape(N, n_g, N_HEADS, HEAD_DIM)
            kv = qkv[..., D_ATTN:].reshape(N, n_g, 2, DIL_KV_HEADS, HEAD_DIM)
            qp, qs = split(q)
            kvp, kvs = split(kv)
            op, os_, new_p, new_s = _dilated_core(qp, kvp, qs, kvs, (state_b_kv1, state_b_kv2, state_b_kv3),
                                                  rel_bias)
            o = jnp.concatenate([op.reshape(NP, D_ATTN), os_.reshape(NS, D_ATTN)], axis=0)
            x = _matmul(o, w_b_o, tm=TM, tn=512, res=x)
            outs['b'] = (new_p, new_s)
        elif kind == 2:
            qkv, qkv_bf = _matmul(h, w_c_qkv, tm=TM, tn=256, bf_copy=True)
            gl = _matmul(h, w_c_gate, tm=TM, tn=3 * N_HEADS)
            kv = qkv[:, D_ATTN:].reshape(N, NSA_KV_ROWS, HEAD_DIM)
            kvp, kvs = split(kv)
            op = _nsa_prompt(qkv, qkv_bf, gl, rel_bias, w_c_cmp1, w_c_cmp2, w_c_cmp_pe, B, S)
            qs = qkv[NP:, :D_ATTN].reshape(DB, T, N_HEADS, HEAD_DIM)
            gs = jax.nn.sigmoid(gl[NP:]).reshape(DB, T, N_HEADS, 3)
            os_, win_s = _nsa_sample_core(qs, kvs, gs, cache_c_kv, state_c_win, page_table, pos_s, rel_bias,
                                          w_c_cmp1, w_c_cmp2, w_c_cmp_pe)
            o = jnp.concatenate([op, os_.reshape(NS, D_ATTN).astype(BF16)], axis=0)
            x = _matmul(o, w_c_o, tm=TM, tn=512, res=x)
            outs['c'] = (kvp[:, :, :4], kvs[:, :, :4], kvp[:, S - min(WIN, S):, 4:6], win_s)
        else:
            w_kvb_r = w_d_kvb.reshape(MLA_KV_LORA, N_HEADS, MLA_NOPE + MLA_V)
            w_uk = w_kvb_r[..., :MLA_NOPE]
            w_uv = w_kvb_r[..., MLA_NOPE:]
            qa = _matmul(h, w_d_qa, tm=TM, tn=512)
            qn = _rmsnorm(qa, g_d_qnorm, BF16, TM)
            qfull = _matmul(qn, w_d_qb, tm=TM, tn=512).reshape(N, N_HEADS, MLA_NOPE + MLA_ROPE)
            kva = _matmul(h, w_d_kva, tm=TM, tn=MLA_KV_LORA + MLA_ROPE)
            pos_all = jnp.concatenate([jnp.tile(pos_p, B), jnp.tile(pos_s, DB)])

            def rope_rows(xr):
                half = xr.shape[-1] // 2
                inv = ROPE_THETA ** (-jnp.arange(half, dtype=F32) / half)
                ang = pos_all.astype(F32)[:, None] * inv[None, :]
                cos = jnp.cos(ang)[:, None, :]
                sin = jnp.sin(ang)[:, None, :]
                x1, x2 = xr[..., :half], xr[..., half:]
                return jnp.concatenate([x1 * cos - x2 * sin, x1 * sin + x2 * cos], -1)

            q_pe = rope_rows(qfull[..., MLA_NOPE:])
            q_lat = jnp.einsum('thn,chn->thc', qfull[..., :MLA_NOPE], w_uk)
            c = _rmsnorm_jax(kva[:, :MLA_KV_LORA], g_d_kvnorm)
            k_pe = rope_rows(kva[:, None, MLA_KV_LORA:])[:, 0, :]
            qcat = jnp.concatenate([q_lat, q_pe], -1)
            lat = jnp.concatenate([c, k_pe], -1)
            qp, qs = split(qcat)
            latp, lats = split(lat)
            op = lax.map(lambda a: _mla_seq(a[0], a[1], pos_p), (qp, latp))

            def sample_seq(a):
                qq, lat_new, pt = a
                lat_all = jnp.concatenate([_gather_pages(cache_d_latent, pt), lat_new], axis=0)
                return _mla_seq(qq, lat_all, pos_s)
            os_ = lax.map(sample_seq, (qs, lats, page_table))
            oc = jnp.concatenate([op.reshape(NP, N_HEADS, MLA_KV_LORA), os_.reshape(NS, N_HEADS, MLA_KV_LORA)], 0)
            o = jnp.einsum('thc,chv->thv', oc, w_uv).reshape(N, N_HEADS * MLA_V)
            x = _matmul(o, w_d_o, tm=TM, tn=512, res=x)
            outs['d'] = (latp, lats)

        h2 = _rmsnorm(x, g_ffn_norm[layer], BF16, TM)
        act = _ffn_in(h2, w_ffn_in, layer, tm=TM, tf=512)
        x = _matmul(act, w_ffn_out_bf, tm=TM, tn=512, res=x, w_layer=layer)

    y = _rmsnorm(x, g_final_norm, F32, TM)
    y_prompt = y[:NP].reshape(B, S, D)
    y_sample = y[NP:].reshape(DB, T, D)
    a_kv_prompt, a_kv_sample = outs['a_kv']
    new_p, new_s = outs['b']
    c_kv_prompt, c_kv_sample, c_win_prompt, c_win_sample = outs['c']
    d_latent_prompt, d_latent_sample = outs['d']
    return (y_prompt, y_sample, a_kv_prompt, a_kv_sample, new_p[0], new_s[0], new_p[1], new_s[1],
            new_p[2], new_s[2], c_kv_prompt, c_kv_sample, c_win_prompt, c_win_sample,
            d_latent_prompt, d_latent_sample)
```

```python
import functools
import math

import jax
import jax.numpy as jnp
import numpy as np
from jax import lax
from jax.experimental import pallas as pl
from jax.experimental.pallas import tpu as pltpu

F32 = jnp.float32
BF16 = jnp.bfloat16

D_MODEL = 2048
DEPTH = 4
PAST_LEN = 8192
HEAD_DIM = 128
N_HEADS = 16
D_ATTN = N_HEADS * HEAD_DIM
ATTN_SCALE = HEAD_DIM ** -0.5
D_FF = 5632
RMS_EPS = 1e-6
REL_BUCKETS = 32
REL_MAX_DIST = 128
NEG_INF = -1e30
FORCE_SCORE = 1e30
GATHER_Q_BLOCK = 16
DENSE_Q_BLOCK = 128
MOBA_BLOCK = 256
MOBA_TOPK = 3
MOBA_KV_HEADS = 2
DIL_PATTERNS = ((128, 1), (512, 4), (2048, 16))
DIL_KV_HEADS = 4
CMP_BLOCK = 32
CMP_STRIDE = 16
SEL_BLOCK = 64
SEL_TOPK = 16
WIN = 512
WIN_BLOCK = 128
NSA_KV_ROWS = 6
MLA_Q_LORA = 512
MLA_KV_LORA = 512
MLA_NOPE = 128
MLA_ROPE = 64
MLA_V = 128
MLA_SCALE = (MLA_NOPE + MLA_ROPE) ** -0.5
ROPE_THETA = 10000.0

VMEM_LIMIT_BYTES = 56 * 1024 * 1024
LANE = 128


def _compiler_params(n_grid_axes):
    return pltpu.CompilerParams(
        dimension_semantics=("arbitrary",) * n_grid_axes,
        vmem_limit_bytes=VMEM_LIMIT_BYTES,
    )


def _rmsnorm_kernel(x_ref, g_ref, o_ref):
    x = x_ref[...]
    ms = jnp.mean(x * x, axis=-1, keepdims=True)
    o_ref[...] = ((x * lax.rsqrt(ms + RMS_EPS)) * g_ref[...]).astype(o_ref.dtype)


def _rmsnorm(x, g, out_dtype, tm):
    n, d = x.shape
    return pl.pallas_call(
        _rmsnorm_kernel,
        grid=(n // tm,),
        in_specs=[pl.BlockSpec((tm, d), lambda i: (i, 0)), pl.BlockSpec((1, d), lambda i: (0, 0))],
        out_specs=pl.BlockSpec((tm, d), lambda i: (i, 0)),
        out_shape=jax.ShapeDtypeStruct((n, d), out_dtype),
        compiler_params=_compiler_params(1),
        name="rmsnorm",
    )(x, g.reshape(1, d))


def _mm_kernel(*refs, has_res, cast_w, bf_copy):
    x_ref, w_ref = refs[0], refs[1]
    rest = refs[2:]
    if has_res:
        r_ref, rest = rest[0], rest[1:]
    o_ref = rest[0]
    rest = rest[1:]
    if bf_copy:
        ob_ref, rest = rest[0], rest[1:]
    if cast_w:
        wb_ref = rest[0]

        @pl.when(pl.program_id(1) == 0)
        def _():
            wb_ref[...] = w_ref[...].astype(BF16)

        w = wb_ref[...]
    else:
        w = w_ref[...]
    acc = jnp.dot(x_ref[...].astype(BF16), w, preferred_element_type=F32)
    if has_res:
        acc = r_ref[...] + acc
    o_ref[...] = acc.astype(o_ref.dtype)
    if bf_copy:
        ob_ref[...] = acc.astype(BF16)


def _matmul(x, w, *, tm, tn, res=None, out_dtype=F32, w_layer=None, w_col0=0, n_out=None, bf_copy=False):
    m, k = x.shape
    n = n_out if n_out is not None else w.shape[-1]
    assert m % tm == 0 and n % tn == 0 and w_col0 % tn == 0
    cast_w = w.dtype != BF16
    cb = w_col0 // tn
    if w_layer is None:
        w_spec = pl.BlockSpec((k, tn), lambda j, i: (0, j + cb))
    else:
        w_spec = pl.BlockSpec((None, k, tn), lambda j, i: (w_layer, 0, j + cb))
    in_specs = [pl.BlockSpec((tm, k), lambda j, i: (i, 0)), w_spec]
    args = [x, w]
    if res is not None:
        in_specs.append(pl.BlockSpec((tm, tn), lambda j, i: (i, j)))
        args.append(res)
    out_spec = pl.BlockSpec((tm, tn), lambda j, i: (i, j))
    out_shape = jax.ShapeDtypeStruct((m, n), out_dtype)
    return pl.pallas_call(
        functools.partial(_mm_kernel, has_res=res is not None, cast_w=cast_w, bf_copy=bf_copy),
        grid=(n // tn, m // tm),
        in_specs=in_specs,
        out_specs=[out_spec, out_spec] if bf_copy else out_spec,
        out_shape=[out_shape, jax.ShapeDtypeStruct((m, n), BF16)] if bf_copy else out_shape,
        scratch_shapes=[pltpu.VMEM((k, tn), BF16)] if cast_w else [],
        compiler_params=_compiler_params(2),
        name="matmul",
    )(*args)


def _ffn_in_kernel(x_ref, wg_ref, wu_ref, o_ref, wgb_ref, wub_ref):
    @pl.when(pl.program_id(1) == 0)
    def _():
        wgb_ref[...] = wg_ref[...].astype(BF16)
        wub_ref[...] = wu_ref[...].astype(BF16)

    x = x_ref[...]
    g = jnp.dot(x, wgb_ref[...], preferred_element_type=F32)
    u = jnp.dot(x, wub_ref[...], preferred_element_type=F32)
    o_ref[...] = ((g * jax.nn.sigmoid(g)) * u).astype(o_ref.dtype)


def _ffn_in(h, w_ffn_in, layer, *, tm, tf):
    m, k = h.shape
    nf = D_FF // tf
    return pl.pallas_call(
        _ffn_in_kernel,
        grid=(nf, m // tm),
        in_specs=[
            pl.BlockSpec((tm, k), lambda j, i: (i, 0)),
            pl.BlockSpec((None, k, tf), lambda j, i: (layer, 0, j)),
            pl.BlockSpec((None, k, tf), lambda j, i: (layer, 0, j + nf)),
        ],
        out_specs=pl.BlockSpec((tm, tf), lambda j, i: (i, j)),
        out_shape=jax.ShapeDtypeStruct((m, D_FF), BF16),
        scratch_shapes=[pltpu.VMEM((k, tf), BF16), pltpu.VMEM((k, tf), BF16)],
        compiler_params=_compiler_params(2),
        name="ffn_in",
    )(h, w_ffn_in, w_ffn_in)


def _np_bucket(dist):
    n = np.maximum(dist, 0)
    max_exact = REL_BUCKETS // 2
    large = max_exact + (np.log(np.maximum(n, 1).astype(np.float32) / np.float32(max_exact))
                         / np.float32(math.log(REL_MAX_DIST / max_exact))
                         * np.float32(REL_BUCKETS - max_exact)).astype(np.int32)
    large = np.minimum(large, REL_BUCKETS - 1)
    return np.where(n < max_exact, n, large).astype(np.int32)


def _bias_table(rel_bias, dist):
    b = rel_bias[jnp.asarray(_np_bucket(dist))]
    return jnp.moveaxis(b, -1, 0)


_NT = (((1,), (1,)), ((), ()))


def _moba_prompt_kernel(q_ref, kbf_ref, vbf_ref, kf_ref, d0_ref, d1_ref, bfar_ref, o_ref, kmean_ref, *, nb, group):
    h = pl.program_id(1)
    qi = pl.program_id(2)
    blk = MOBA_BLOCK

    @pl.when((qi == 0) & (h % group == 0))
    def _():
        kf = kf_ref[...]
        kmean_ref[...] = jnp.mean(kf.reshape(nb, blk, HEAD_DIM), axis=1).astype(BF16)

    q = q_ref[...]
    gate = lax.dot_general(q, kmean_ref[...], _NT, preferred_element_type=F32)
    col = lax.broadcasted_iota(jnp.int32, gate.shape, 1)
    colf = col.astype(F32)
    g = jnp.where(col < qi, gate, NEG_INF)
    sel = jnp.zeros(gate.shape, F32)
    for _ in range(MOBA_TOPK):
        mx = jnp.max(g, axis=-1, keepdims=True)
        idx = jnp.min(jnp.where(g == mx, colf, float(nb)), axis=-1, keepdims=True)
        pick = colf == idx
        sel = jnp.where(pick, 1.0, sel)
        g = jnp.where(pick, -jnp.inf, g)
    sel = jnp.where(col < qi, sel, 0.0)

    def logits(n):
        start = pl.multiple_of(n * blk, blk)
        k = kbf_ref[pl.ds(start, blk), :]
        return lax.dot_general(q, k, _NT, preferred_element_type=F32) * ATTN_SCALE

    def values(n):
        start = pl.multiple_of(n * blk, blk)
        return vbf_ref[pl.ds(start, blk), :]

    row = lax.broadcasted_iota(jnp.int32, (blk, blk), 0)
    colk = lax.broadcasted_iota(jnp.int32, (blk, blk), 1)
    s = jnp.where(row >= colk, logits(qi) + d0_ref[...], NEG_INF)
    m = jnp.max(s, axis=-1, keepdims=True)
    p = jnp.exp(s - m)
    l = jnp.sum(p, axis=-1, keepdims=True)
    acc = jnp.dot(p.astype(BF16), values(qi), preferred_element_type=F32)

    def body(n, carry):
        m, l, acc = carry
        chosen = jnp.sum(jnp.where(col == n, sel, 0.0), axis=-1, keepdims=True) > 0.5
        bias = jnp.where(n == qi - 1, d1_ref[...], bfar_ref[0:1, :])
        s = jnp.where(chosen, logits(n) + bias, NEG_INF)
        m_new = jnp.maximum(m, jnp.max(s, axis=-1, keepdims=True))
        alpha = jnp.exp(m - m_new)
        p = jnp.exp(s - m_new)
        l = alpha * l + jnp.sum(p, axis=-1, keepdims=True)
        acc = alpha * acc + jnp.dot(p.astype(BF16), values(n), preferred_element_type=F32)
        return m_new, l, acc

    m, l, acc = lax.fori_loop(0, qi, body, (m, l, acc))
    o_ref[...] = (acc / l).astype(o_ref.dtype)


def _moba_prompt(qkv, qkv_bf, rel_bias, B, S):
    blk = MOBA_BLOCK
    nb = S // blk
    group = N_HEADS // MOBA_KV_HEADS
    qcols = D_ATTN // HEAD_DIM
    i = np.arange(blk)
    d0 = _bias_table(rel_bias, i[:, None] - i[None, :])
    d1 = _bias_table(rel_bias, blk + i[:, None] - i[None, :])
    bfar = jnp.broadcast_to(rel_bias[_np_bucket(np.array(2 * blk))][:, None, None], (N_HEADS, 8, blk))
    return pl.pallas_call(
        functools.partial(_moba_prompt_kernel, nb=nb, group=group),
        grid=(B, N_HEADS, nb),
        in_specs=[
            pl.BlockSpec((blk, HEAD_DIM), lambda b, h, i: (b * nb + i, h)),
            pl.BlockSpec((S, HEAD_DIM), lambda b, h, i: (b, qcols + h // group)),
            pl.BlockSpec((S, HEAD_DIM), lambda b, h, i: (b, qcols + MOBA_KV_HEADS + h // group)),
            pl.BlockSpec((S, HEAD_DIM), lambda b, h, i: (b, qcols + h // group)),
            pl.BlockSpec((None, blk, blk), lambda b, h, i: (h, 0, 0)),
            pl.BlockSpec((None, blk, blk), lambda b, h, i: (h, 0, 0)),
            pl.BlockSpec((None, 8, blk), lambda b, h, i: (h, 0, 0)),
        ],
        out_specs=pl.BlockSpec((blk, HEAD_DIM), lambda b, h, i: (b * nb + i, h)),
        out_shape=jax.ShapeDtypeStruct((B * S, D_ATTN), BF16),
        scratch_shapes=[pltpu.VMEM((nb, HEAD_DIM), BF16)],
        compiler_params=_compiler_params(3),
        name="moba_prompt",
    )(qkv_bf, qkv_bf, qkv_bf, qkv, d0, d1, bfar)


PAGES_PER_STEP = 8
NSA_HEADS_PER_STEP = 4


def _moba_sample_kernel(pt_ref, x_ref, bias_ref, b0_ref, *rest, group):
    pages = rest[:PAGES_PER_STEP]
    o_ref, g_scr, m_scr, l_scr, o_scr = rest[PAGES_PER_STEP:]
    j = pl.program_id(1)
    nsteps = pl.num_programs(1)
    pages_per_blk = MOBA_BLOCK // pages[0].shape[0]
    blks_per_step = PAGES_PER_STEP // pages_per_blk
    x = x_ref[...]
    q_bf = x[:N_HEADS].astype(BF16)
    qf = q_bf.astype(F32)
    lanes = (group, HEAD_DIM)
    for kvh in range(MOBA_KV_HEADS):
        rows = slice(kvh * group, (kvh + 1) * group)
        for bi in range(blks_per_step):
            n = j * blks_per_step + bi
            prs = pages[bi * pages_per_blk:(bi + 1) * pages_per_blk]
            k = jnp.concatenate([p[:, 0, kvh, :] for p in prs], axis=0)
            v = jnp.concatenate([p[:, 1, kvh, :] for p in prs], axis=0)
            kmean = jnp.mean(k, axis=0, keepdims=True).astype(BF16).astype(F32)
            gate = jnp.sum(qf[rows] * kmean, axis=-1, keepdims=True)
            s = (lax.dot_general(q_bf[rows], k.astype(BF16), _NT, preferred_element_type=F32) * ATTN_SCALE
                 + bias_ref[n, rows, :])
            m = jnp.max(s, axis=-1, keepdims=True)
            p = jnp.exp(s - m)
            l = jnp.sum(p, axis=-1, keepdims=True)
            o = jnp.dot(p.astype(BF16), v.astype(BF16), preferred_element_type=F32)
            g_scr[n, rows, :] = jnp.broadcast_to(gate, lanes)
            m_scr[n, rows, :] = jnp.broadcast_to(m, lanes)
            l_scr[n, rows, :] = jnp.broadcast_to(l, lanes)
            o_scr[n, rows, :] = o

    @pl.when(j == nsteps - 1)
    def _():
        g = g_scr[...]
        nidx = lax.broadcasted_iota(jnp.int32, g.shape, 0).astype(F32)
        sel = jnp.zeros(g.shape, F32)
        for _ in range(MOBA_TOPK):
            mx = jnp.max(g, axis=0, keepdims=True)
            idx = jnp.min(jnp.where(g == mx, nidx, float(g.shape[0])), axis=0, keepdims=True)
            pick = nidx == idx
            sel = jnp.where(pick, 1.0, sel)
            g = jnp.where(pick, -jnp.inf, g)
        chosen = sel > 0.5
        k_new = x[N_HEADS:N_HEADS + MOBA_KV_HEADS].astype(BF16).astype(F32)
        v_new = x[N_HEADS + MOBA_KV_HEADS:N_HEADS + 2 * MOBA_KV_HEADS].astype(BF16).astype(F32)
        k_rows = jnp.concatenate([jnp.broadcast_to(k_new[i:i + 1], lanes) for i in range(MOBA_KV_HEADS)], axis=0)
        v_rows = jnp.concatenate([jnp.broadcast_to(v_new[i:i + 1], lanes) for i in range(MOBA_KV_HEADS)], axis=0)
        s_new = jnp.sum(qf * k_rows, axis=-1, keepdims=True) * ATTN_SCALE + b0_ref[...]
        m_blk = m_scr[...]
        m_tot = jnp.maximum(jnp.max(jnp.where(chosen, m_blk, NEG_INF), axis=0), s_new)
        w = jnp.where(chosen, jnp.exp(m_blk - m_tot[None]), 0.0)
        e_new = jnp.exp(s_new - m_tot)
        l_tot = jnp.sum(w * l_scr[...], axis=0) + e_new
        o_ref[...] = (jnp.sum(w * o_scr[...], axis=0) + e_new * v_rows) / l_tot


def _moba_sample(x_s, cache_a_kv, page_table, rel_bias):
    DB, n_pages = page_table.shape
    n_pool, page = cache_a_kv.shape[:2]
    group = N_HEADS // MOBA_KV_HEADS
    nblk = n_pages * page // MOBA_BLOCK
    kpos = np.arange(nblk * MOBA_BLOCK).reshape(nblk, MOBA_BLOCK)
    bias = jnp.transpose(_bias_table(rel_bias, PAST_LEN - kpos), (1, 0, 2))
    b0 = jnp.broadcast_to(rel_bias[0][:, None], (N_HEADS, HEAD_DIM))
    nsteps = n_pages // PAGES_PER_STEP
    rows = x_s.shape[1]

    def page_spec(i):
        return pl.BlockSpec((None, page, 2, MOBA_KV_HEADS, HEAD_DIM),
                            lambda b, j, pt: (pt[b, j * PAGES_PER_STEP + i], 0, 0, 0, 0))

    grid_spec = pltpu.PrefetchScalarGridSpec(
        num_scalar_prefetch=1,
        grid=(DB, nsteps),
        in_specs=[
            pl.BlockSpec((None, rows, HEAD_DIM), lambda b, j, pt: (b, 0, 0)),
            pl.BlockSpec((nblk, N_HEADS, MOBA_BLOCK), lambda b, j, pt: (0, 0, 0)),
            pl.BlockSpec((N_HEADS, HEAD_DIM), lambda b, j, pt: (0, 0)),
        ] + [page_spec(i) for i in range(PAGES_PER_STEP)],
        out_specs=pl.BlockSpec((None, N_HEADS, HEAD_DIM), lambda b, j, pt: (b, 0, 0)),
        scratch_shapes=[pltpu.VMEM((nblk, N_HEADS, HEAD_DIM), F32) for _ in range(4)],
    )
    return pl.pallas_call(
        functools.partial(_moba_sample_kernel, group=group),
        grid_spec=grid_spec,
        out_shape=jax.ShapeDtypeStruct((DB, N_HEADS, HEAD_DIM), F32),
        compiler_params=_compiler_params(2),
        name="moba_sample",
    )(page_table, x_s, bias, b0, *([cache_a_kv] * PAGES_PER_STEP))


DIL_R = DIL_PATTERNS[0][0] // DIL_PATTERNS[0][1]
DIL_WIDTH = D_ATTN + 2 * DIL_KV_HEADS * HEAD_DIM


def _dilated_prompt_kernel(q_ref, k_ref, v_ref, dcur_ref, dprev_ref, o_ref, *, heads):
    i = pl.program_id(3)
    R = DIL_R
    row = lax.broadcasted_iota(jnp.int32, (R, R), 0)
    col = lax.broadcasted_iota(jnp.int32, (R, R), 1)
    has_prev = jnp.full((R, R), i, jnp.int32) > 0
    cur = pl.ds(pl.multiple_of(i * R, R), R)
    prev = pl.ds(pl.multiple_of(jnp.maximum(i - 1, 0) * R, R), R)
    k_cur, v_cur, k_prev, v_prev = k_ref[cur, :], v_ref[cur, :], k_ref[prev, :], v_ref[prev, :]
    for hh in range(heads):
        qh = q_ref[:, hh * HEAD_DIM:(hh + 1) * HEAD_DIM]
        s = lax.dot_general(qh, k_cur, _NT, preferred_element_type=F32) * ATTN_SCALE + dcur_ref[hh]
        s = jnp.where(row >= col, s, NEG_INF)
        m = jnp.max(s, axis=-1, keepdims=True)
        p = jnp.exp(s - m)
        l = jnp.sum(p, axis=-1, keepdims=True)
        acc = jnp.dot(p.astype(BF16), v_cur, preferred_element_type=F32)
        m, l, acc = _flash_update(qh, k_prev, v_prev, dprev_ref[hh], (col >= row) & has_prev, m, l, acc)
        base = hh * 2 * HEAD_DIM
        o_ref[:, base:base + HEAD_DIM] = acc / l
        o_ref[:, base + HEAD_DIM:base + 2 * HEAD_DIM] = jnp.broadcast_to(m + jnp.log(l), (R, HEAD_DIM))


def _dilated_prompt_group(qkv_bf, rel_bias, gi, B, S):
    win, dil = DIL_PATTERNS[gi]
    R = DIL_R
    G = N_HEADS // DIL_KV_HEADS
    n_rows, n_cols = qkv_bf.shape
    nt = S // dil // R
    a = np.arange(R)
    dcur = _bias_table(rel_bias, (a[:, None] - a[None, :]) * dil).reshape(DIL_KV_HEADS, G, R, R)
    dprev = _bias_table(rel_bias, (R + a[:, None] - a[None, :]) * dil).reshape(DIL_KV_HEADS, G, R, R)
    view = qkv_bf.reshape(n_rows // dil, dil * n_cols)
    qw = G * HEAD_DIM
    q_blk0 = gi * DIL_WIDTH // qw
    k_blk0 = (gi * DIL_WIDTH + D_ATTN) // HEAD_DIM
    out_w = G * 2 * HEAD_DIM
    return pl.pallas_call(
        functools.partial(_dilated_prompt_kernel, heads=G),
        grid=(B, dil, DIL_KV_HEADS, nt),
        in_specs=[
            pl.BlockSpec((R, qw), lambda b, r, k, i: (b * nt + i, r * (n_cols // qw) + q_blk0 + k)),
            pl.BlockSpec((S // dil, HEAD_DIM), lambda b, r, k, i: (b, r * (n_cols // HEAD_DIM) + k_blk0 + k)),
            pl.BlockSpec((S // dil, HEAD_DIM),
                         lambda b, r, k, i: (b, r * (n_cols // HEAD_DIM) + k_blk0 + DIL_KV_HEADS + k)),
            pl.BlockSpec((None, G, R, R), lambda b, r, k, i: (k, 0, 0, 0)),
            pl.BlockSpec((None, G, R, R), lambda b, r, k, i: (k, 0, 0, 0)),
        ],
        out_specs=pl.BlockSpec((R, out_w), lambda b, r, k, i: (b * nt + i, r * DIL_KV_HEADS + k)),
        out_shape=jax.ShapeDtypeStruct((B * S // dil, dil * N_HEADS * 2 * HEAD_DIM), F32),
        compiler_params=_compiler_params(4),
        name="dilated_prompt",
    )(view, view, view, dcur, dprev).reshape(B * S, N_HEADS * 2 * HEAD_DIM)


def _dilated_merge_kernel(a_ref, b_ref, c_ref, o_ref):
    for h in range(N_HEADS):
        oc = slice(h * 2 * HEAD_DIM, h * 2 * HEAD_DIM + HEAD_DIM)
        lc = slice(h * 2 * HEAD_DIM + HEAD_DIM, (h + 1) * 2 * HEAD_DIM)
        lses = [r[:, lc] for r in (a_ref, b_ref, c_ref)]
        m = jnp.maximum(jnp.maximum(lses[0], lses[1]), lses[2])
        es = [jnp.exp(x - m) for x in lses]
        tot = es[0] + es[1] + es[2]
        o = (es[0] / tot) * a_ref[:, oc] + (es[1] / tot) * b_ref[:, oc] + (es[2] / tot) * c_ref[:, oc]
        o_ref[:, h * HEAD_DIM:(h + 1) * HEAD_DIM] = o.astype(o_ref.dtype)


def _dilated_merge(parts, tm):
    n, w = parts[0].shape
    spec = pl.BlockSpec((tm, w), lambda i: (i, 0))
    return pl.pallas_call(
        _dilated_merge_kernel,
        grid=(n // tm,),
        in_specs=[spec, spec, spec],
        out_specs=pl.BlockSpec((tm, D_ATTN), lambda i: (i, 0)),
        out_shape=jax.ShapeDtypeStruct((n, D_ATTN), BF16),
        compiler_params=_compiler_params(1),
        name="dilated_merge",
    )(*parts)


def _dilated_sample_kernel(x_ref, bias_ref, b1_ref, b2_ref, b3_ref, o_ref):
    G = N_HEADS // DIL_KV_HEADS
    rows_per_group = N_HEADS + 2 * DIL_KV_HEADS
    x = x_ref[...]
    for k in range(DIL_KV_HEADS):
        outs, lses = [], []
        for gi, buf in enumerate((b1_ref, b2_ref, b3_ref)):
            base = gi * rows_per_group
            q = x[base + k * G:base + (k + 1) * G].astype(BF16)
            k_new = x[base + N_HEADS + k:base + N_HEADS + k + 1].astype(BF16).astype(F32)
            v_new = x[base + N_HEADS + DIL_KV_HEADS + k:base + N_HEADS + DIL_KV_HEADS + k + 1].astype(BF16).astype(F32)
            hs = slice(k * G, (k + 1) * G)
            s = (lax.dot_general(q, buf[:, 0, k, :].astype(BF16), _NT, preferred_element_type=F32) * ATTN_SCALE
                 + bias_ref[gi, hs, :DIL_R])
            s_new = (jnp.sum(q.astype(F32) * k_new, axis=-1, keepdims=True) * ATTN_SCALE
                     + bias_ref[gi, hs, DIL_R:DIL_R + 1])
            m = jnp.maximum(jnp.max(s, axis=-1, keepdims=True), s_new)
            p = jnp.exp(s - m)
            e_new = jnp.exp(s_new - m)
            l = jnp.sum(p, axis=-1, keepdims=True) + e_new
            outs.append((jnp.dot(p.astype(BF16), buf[:, 1, k, :].astype(BF16), preferred_element_type=F32)
                         + e_new * v_new) / l)
            lses.append(m + jnp.log(l))
        m = jnp.maximum(jnp.maximum(lses[0], lses[1]), lses[2])
        es = [jnp.exp(z - m) for z in lses]
        tot = es[0] + es[1] + es[2]
        o_ref[k * G:(k + 1) * G, :] = (es[0] / tot) * outs[0] + (es[1] / tot) * outs[1] + (es[2] / tot) * outs[2]


def _dilated_sample(x_s, bufs, rel_bias):
    DB, rows, _ = x_s.shape
    R = DIL_R
    m = np.arange(R + 1)
    steps = np.concatenate([R - np.arange(R), [0]])
    bias = jnp.stack([jnp.pad(_bias_table(rel_bias, steps * dil), ((0, 0), (0, 2 * LANE - R - 1)))
                      for _, dil in DIL_PATTERNS])
    views, specs = [], []
    for buf, (win, dil) in zip(bufs, DIL_PATTERNS):
        assert buf.shape[1] == win
        views.append(buf.reshape((DB, R, dil) + buf.shape[2:]))
        specs.append(pl.BlockSpec((None, R, None) + buf.shape[2:], lambda b: (b, 0, 0, 0, 0, 0)))
    return pl.pallas_call(
        _dilated_sample_kernel,
        grid=(DB,),
        in_specs=[pl.BlockSpec((None, rows, HEAD_DIM), lambda b: (b, 0, 0)),
                  pl.BlockSpec(bias.shape, lambda b: (0, 0, 0))] + specs,
        out_specs=pl.BlockSpec((None, N_HEADS, HEAD_DIM), lambda b: (b, 0, 0)),
        out_shape=jax.ShapeDtypeStruct((DB, N_HEADS, HEAD_DIM), F32),
        compiler_params=_compiler_params(1),
        name="dilated_sample",
    )(x_s, bias, *views)


def _bucket_thresholds():
    table = _np_bucket(np.arange(2 * REL_MAX_DIST))
    return [int(np.argmax(table >= b)) for b in range(REL_BUCKETS)]


def _bias_from_dist(dist, tab_ref, h):
    thr = _bucket_thresholds()
    bias = jnp.full(dist.shape, tab_ref[h, 0], F32)
    for b in range(1, REL_BUCKETS):
        bias = jnp.where(dist >= thr[b], tab_ref[h, b], bias)
    return bias


def _compress_kernel(a_ref, pe_ref, w1_ref, w2_ref, o_ref):
    half = CMP_STRIDE * HEAD_DIM
    a = a_ref[...]
    a_lo = (a + pe_ref[:, :half]).astype(BF16)
    a_hi = (a + pe_ref[:, half:]).astype(BF16)
    w1 = w1_ref[...].astype(BF16)
    p_lo = jnp.dot(a_lo, w1[:half], preferred_element_type=F32)
    p_hi = jnp.dot(a_hi, w1[half:], preferred_element_type=F32)
    nch = a.shape[0]
    hid = jax.nn.gelu(p_lo + pltpu.roll(p_hi, nch - 1, 0))
    o_ref[...] = jnp.dot(hid.astype(BF16), w2_ref[...].astype(BF16), preferred_element_type=F32).astype(o_ref.dtype)


def _compress_call(chunks, w1, w2, pe):
    G, _, nch, width = chunks.shape
    pe_flat = pe.reshape(2, 1, CMP_BLOCK * HEAD_DIM)
    return pl.pallas_call(
        _compress_kernel,
        grid=(2, G),
        in_specs=[
            pl.BlockSpec((None, None, nch, width), lambda c, g: (g, c, 0, 0)),
            pl.BlockSpec((None, 1, CMP_BLOCK * HEAD_DIM), lambda c, g: (c, 0, 0)),
            pl.BlockSpec((None, CMP_BLOCK * HEAD_DIM, w1.shape[-1]), lambda c, g: (c, 0, 0)),
            pl.BlockSpec((None, w2.shape[1], HEAD_DIM), lambda c, g: (c, 0, 0)),
        ],
        out_specs=pl.BlockSpec((None, None, nch, HEAD_DIM), lambda c, g: (g, c, 0, 0)),
        out_shape=jax.ShapeDtypeStruct((G, 2, nch, HEAD_DIM), BF16),
        compiler_params=_compiler_params(2),
        name="nsa_compress",
    )(chunks, pe_flat, w1, w2)


def _nsa_cmp_kernel(tab_ref, q_ref, gl_ref, kc_ref, vc_ref, ov_ref, o_ref, sel_ref, *, pos0, tq):
    qt = pl.program_id(1)
    ncmp = kc_ref.shape[0]
    nsel = ov_ref.shape[1]
    pos = pos0 + qt * tq + lax.broadcasted_iota(jnp.int32, (tq, 1), 0)
    ncol = lax.broadcasted_iota(jnp.int32, (tq, ncmp), 1)
    dc = pos - (CMP_STRIDE * ncol + CMP_BLOCK - 1)
    valid = dc >= 0
    kc = kc_ref[...]
    vc = vc_ref[...]
    gates = jax.nn.sigmoid(gl_ref[...])
    pcsum = jnp.zeros((tq, ncmp), F32)
    for h in range(N_HEADS):
        qh = q_ref[:, h * HEAD_DIM:(h + 1) * HEAD_DIM]
        lc = lax.dot_general(qh, kc, _NT, preferred_element_type=F32) * ATTN_SCALE + _bias_from_dist(dc, tab_ref, h)
        lc = jnp.where(valid, lc, NEG_INF)
        m = jnp.max(lc, axis=-1, keepdims=True)
        e = jnp.where(valid, jnp.exp(lc - m), 0.0)
        s = jnp.maximum(jnp.sum(e, axis=-1, keepdims=True), 1e-30)
        pc = e / s
        pcsum = pcsum + pc
        o_cmp = jnp.dot(pc.astype(BF16), vc, preferred_element_type=F32)
        o_ref[:, h * HEAD_DIM:(h + 1) * HEAD_DIM] = gates[:, 3 * h:3 * h + 1] * o_cmp
    imp = jnp.dot(pcsum.astype(BF16), ov_ref[...], preferred_element_type=F32)
    j = lax.broadcasted_iota(jnp.int32, (tq, nsel), 1)
    cur = jnp.right_shift(pos, int(math.log2(SEL_BLOCK)))
    forced = (j == 0) | (j == cur) | (j == cur - 1)
    score = jnp.where(j > cur, NEG_INF, jnp.where(forced, FORCE_SCORE, imp))
    rank = jnp.zeros((tq, nsel), F32)
    for jp in range(nsel):
        sj = score[:, jp:jp + 1]
        ahead = (sj > score) | ((sj == score) & (j > jp))
        rank = rank + jnp.where(ahead, 1.0, 0.0)
    sel_ref[...] = jnp.where(rank < float(min(SEL_TOPK, nsel)), 1.0, 0.0).astype(sel_ref.dtype)


def _flash_update(qh, k, v, bias, valid, m, l, acc):
    s = lax.dot_general(qh, k, _NT, preferred_element_type=F32) * ATTN_SCALE + bias
    s = jnp.where(valid, s, NEG_INF)
    m_new = jnp.maximum(m, jnp.max(s, axis=-1, keepdims=True))
    alpha = jnp.exp(m - m_new)
    p = jnp.exp(s - m_new)
    l = alpha * l + jnp.sum(p, axis=-1, keepdims=True)
    acc = alpha * acc + jnp.dot(p.astype(BF16), v, preferred_element_type=F32)
    return m_new, l, acc


def _nsa_selwin_kernel(q_ref, ks_ref, vs_ref, kw_ref, vw_ref, selm_ref, e_ref, gl_ref, ocmp_ref, d0_ref, d1_ref,
                       bfar_ref, o_ref, kmask_ref, *, tq, hp):
    qt = pl.program_id(1)
    hg = pl.program_id(2)

    @pl.when(hg == 0)
    def _():
        kmask_ref[...] = jnp.dot(selm_ref[...], e_ref[...], preferred_element_type=F32)

    qs = [q_ref[:, i * HEAD_DIM:(i + 1) * HEAD_DIM] for i in range(hp)]
    row = lax.broadcasted_iota(jnp.int32, (tq, tq), 0)
    colk = lax.broadcasted_iota(jnp.int32, (tq, tq), 1)
    causal = row >= colk

    def tile(ref, t):
        return ref[pl.ds(pl.multiple_of(t * tq, tq), tq), :]

    def kmask(t):
        return kmask_ref[:, pl.ds(pl.multiple_of(t * tq, tq), tq)] > 0.5

    def first(i, k, v, valid):
        s = lax.dot_general(qs[i], k, _NT, preferred_element_type=F32) * ATTN_SCALE + d0_ref[i]
        s = jnp.where(valid, s, NEG_INF)
        m = jnp.max(s, axis=-1, keepdims=True)
        p = jnp.exp(s - m)
        return m, jnp.sum(p, axis=-1, keepdims=True), jnp.dot(p.astype(BF16), v, preferred_element_type=F32)

    k_own, v_own, own_valid = tile(ks_ref, qt), tile(vs_ref, qt), causal & kmask(qt)
    states = tuple(first(i, k_own, v_own, own_valid) for i in range(hp))

    def sel_body(t, sts):
        k_t, v_t, chosen, is_prev = tile(ks_ref, t), tile(vs_ref, t), kmask(t), t == qt - 1
        return tuple(_flash_update(qs[i], k_t, v_t, jnp.where(is_prev, d1_ref[i], bfar_ref[i, 0:1, :]), chosen,
                                   *sts[i]) for i in range(hp))

    states = lax.fori_loop(0, qt, sel_body, states)
    o_sel = [acc / l for _, l, acc in states]

    k_own, v_own = tile(kw_ref, qt), tile(vw_ref, qt)
    states = [first(i, k_own, v_own, causal) for i in range(hp)]
    n_back = WIN // tq
    for delta in range(1, n_back + 1):
        t = jnp.maximum(qt - delta, 0)
        k_t, v_t = tile(kw_ref, t), tile(vw_ref, t)
        inside = jnp.full((tq, tq), qt, jnp.int32) >= delta
        valid = inside if delta < n_back else (row <= colk) & inside
        for i in range(hp):
            bias = d1_ref[i] if delta == 1 else bfar_ref[i, 0:1, :]
            states[i] = _flash_update(qs[i], k_t, v_t, bias, valid, *states[i])
    o_win = [acc / l for _, l, acc in states]

    gates = jax.nn.sigmoid(gl_ref[...])
    gcol = lax.broadcasted_iota(jnp.int32, gates.shape, 1)
    for i in range(hp):
        h = hg * hp + i
        g_sel = jnp.sum(jnp.where(gcol == 3 * h + 1, gates, 0.0), axis=-1, keepdims=True)
        g_win = jnp.sum(jnp.where(gcol == 3 * h + 2, gates, 0.0), axis=-1, keepdims=True)
        cols = slice(i * HEAD_DIM, (i + 1) * HEAD_DIM)
        o_ref[:, cols] = ((ocmp_ref[:, cols] + g_sel * o_sel[i]) + g_win * o_win[i]).astype(o_ref.dtype)


def _nsa_prompt(qkv, qkv_bf, gl, rel_bias, w_cmp1, w_cmp2, w_cmp_pe, B, S):
    tq = WIN_BLOCK
    nqt = S // tq
    nch = S // CMP_STRIDE
    nsel = S // SEL_BLOCK
    qcols = D_ATTN // HEAD_DIM
    kv_cmp = qkv[:B * S, D_ATTN:D_ATTN + 2 * HEAD_DIM].reshape(B, S, 2, HEAD_DIM)
    chunks = kv_cmp.transpose(0, 2, 1, 3).reshape(B, 2, nch, CMP_STRIDE * HEAD_DIM)
    cmp = _compress_call(chunks, w_cmp1, w_cmp2, w_cmp_pe)
    n = np.arange(nch)[:, None]
    jj = np.arange(nsel)[None, :]
    overlap = ((n * CMP_STRIDE < jj * SEL_BLOCK + SEL_BLOCK) & (n * CMP_STRIDE + CMP_BLOCK > jj * SEL_BLOCK)
               & (n < nch - 1))
    ov = jnp.asarray(overlap, BF16)
    expand = jnp.asarray(np.arange(S)[None, :] // SEL_BLOCK == np.arange(nsel)[:, None], BF16)
    tab = rel_bias.T
    ocmp, selm = pl.pallas_call(
        functools.partial(_nsa_cmp_kernel, pos0=0, tq=tq),
        grid=(B, nqt),
        in_specs=[
            pl.BlockSpec(memory_space=pltpu.SMEM),
            pl.BlockSpec((tq, D_ATTN), lambda b, t: (b * nqt + t, 0)),
            pl.BlockSpec((tq, 3 * N_HEADS), lambda b, t: (b * nqt + t, 0)),
            pl.BlockSpec((None, None, nch, HEAD_DIM), lambda b, t: (b, 0, 0, 0)),
            pl.BlockSpec((None, None, nch, HEAD_DIM), lambda b, t: (b, 1, 0, 0)),
            pl.BlockSpec((nch, nsel), lambda b, t: (0, 0)),
        ],
        out_specs=[
            pl.BlockSpec((tq, D_ATTN), lambda b, t: (b * nqt + t, 0)),
            pl.BlockSpec((tq, nsel), lambda b, t: (b * nqt + t, 0)),
        ],
        out_shape=[jax.ShapeDtypeStruct((B * S, D_ATTN), F32), jax.ShapeDtypeStruct((B * S, nsel), BF16)],
        compiler_params=_compiler_params(2),
        name="nsa_cmp",
    )(tab, qkv_bf, gl, cmp, cmp, ov)

    i = np.arange(tq)
    d0 = _bias_table(rel_bias, i[:, None] - i[None, :])
    d1 = _bias_table(rel_bias, tq + i[:, None] - i[None, :])
    bfar = jnp.broadcast_to(rel_bias[_np_bucket(np.array(2 * tq))][:, None, None], (N_HEADS, 8, tq))
    seq_spec = lambda c: pl.BlockSpec((S, HEAD_DIM), lambda b, t, h: (b, qcols + c))
    tile_spec = lambda w: pl.BlockSpec((tq, w), lambda b, t, h: (b * nqt + t, 0))
    hp = NSA_HEADS_PER_STEP
    head_tile = pl.BlockSpec((tq, hp * HEAD_DIM), lambda b, t, h: (b * nqt + t, h))
    return pl.pallas_call(
        functools.partial(_nsa_selwin_kernel, tq=tq, hp=hp),
        grid=(B, nqt, N_HEADS // hp),
        in_specs=[
            head_tile, seq_spec(2), seq_spec(3), seq_spec(4), seq_spec(5),
            tile_spec(nsel),
            pl.BlockSpec((nsel, S), lambda b, t, h: (0, 0)),
            tile_spec(3 * N_HEADS),
            head_tile,
            pl.BlockSpec((hp, tq, tq), lambda b, t, h: (h, 0, 0)),
            pl.BlockSpec((hp, tq, tq), lambda b, t, h: (h, 0, 0)),
            pl.BlockSpec((hp, 8, tq), lambda b, t, h: (h, 0, 0)),
        ],
        out_specs=head_tile,
        out_shape=jax.ShapeDtypeStruct((B * S, D_ATTN), BF16),
        scratch_shapes=[pltpu.VMEM((tq, S), F32)],
        compiler_params=_compiler_params(3),
        name="nsa_selwin",
    )(qkv_bf, qkv_bf, qkv_bf, qkv_bf, qkv_bf, selm, expand, gl, ocmp, d0, d1, bfar)


def _nsa_sample_cmp_kernel(pt_ref, x_ref, gl_ref, biasc_ref, ov_ref, w1_ref, w2_ref, pe_ref, *rest, n_cmp):
    pages = rest[:PAGES_PER_STEP]
    ocmp_ref, selm_ref, seq_scr = rest[PAGES_PER_STEP:]
    j = pl.program_id(1)
    nsteps = pl.num_programs(1)
    page = pages[0].shape[0]
    for i, p in enumerate(pages):
        start = pl.multiple_of((j * PAGES_PER_STEP + i) * page, page)
        for c in range(2):
            seq_scr[c, pl.ds(start, page), :] = p[:, c, :]

    @pl.when(j == nsteps - 1)
    def _():
        nch = seq_scr.shape[1] // CMP_STRIDE
        hidden = w1_ref.shape[-1]
        cmp = []
        for c in range(2):
            p_lo = jnp.zeros((nch, hidden), F32)
            p_hi = jnp.zeros((nch, hidden), F32)
            for r in range(CMP_STRIDE):
                rows = seq_scr[c, pl.ds(r, nch, stride=CMP_STRIDE), :]
                a_lo = (rows + pe_ref[c, r:r + 1, :]).astype(BF16)
                a_hi = (rows + pe_ref[c, CMP_STRIDE + r:CMP_STRIDE + r + 1, :]).astype(BF16)
                p_lo = p_lo + jnp.dot(a_lo, w1_ref[c, r * HEAD_DIM:(r + 1) * HEAD_DIM, :],
                                      preferred_element_type=F32)
                p_hi = p_hi + jnp.dot(a_hi, w1_ref[c, (CMP_STRIDE + r) * HEAD_DIM:(CMP_STRIDE + r + 1) * HEAD_DIM, :],
                                      preferred_element_type=F32)
            hid = jax.nn.gelu(p_lo + pltpu.roll(p_hi, nch - 1, 0))
            cmp.append(jnp.dot(hid.astype(BF16), w2_ref[c].astype(BF16), preferred_element_type=F32).astype(BF16))
        kc, vc = cmp
        x = x_ref[...]
        q = x[:N_HEADS].astype(BF16)
        ncol = lax.broadcasted_iota(jnp.int32, (N_HEADS, nch), 1)
        valid = ncol < n_cmp
        lc = lax.dot_general(q, kc, _NT, preferred_element_type=F32) * ATTN_SCALE + biasc_ref[...]
        lc = jnp.where(valid, lc, NEG_INF)
        m = jnp.max(lc, axis=-1, keepdims=True)
        e = jnp.where(valid, jnp.exp(lc - m), 0.0)
        pc = e / jnp.maximum(jnp.sum(e, axis=-1, keepdims=True), 1e-30)
        gates = jax.nn.sigmoid(gl_ref[...])
        ocmp_ref[...] = gates[:, 0:1] * jnp.dot(pc.astype(BF16), vc, preferred_element_type=F32)
        pcsum = jnp.broadcast_to(jnp.sum(pc, axis=0, keepdims=True), (8, nch))
        imp = jnp.dot(pcsum.astype(BF16), ov_ref[...], preferred_element_type=F32)
        jcol = lax.broadcasted_iota(jnp.int32, imp.shape, 1)
        cur = PAST_LEN // SEL_BLOCK
        forced = (jcol == 0) | (jcol == cur) | (jcol == cur - 1)
        score = jnp.where(jcol > cur, NEG_INF, jnp.where(forced, FORCE_SCORE, imp))
        rank = jnp.zeros(imp.shape, F32)
        for jp in range(cur + 1):
            sj = score[:, jp:jp + 1]
            ahead = (sj > score) | ((sj == score) & (jcol > jp))
            rank = rank + jnp.where(ahead, 1.0, 0.0)
        selm_ref[...] = jnp.where((rank < float(SEL_TOPK)) & (jcol <= cur), 1.0, 0.0)


def _nsa_sample_sel_kernel(info_ref, x_ref, gl_ref, ocmp_ref, tab_ref, biasw_ref, win_ref, *rest):
    blocks = rest[:SEL_TOPK]
    o_ref = rest[SEL_TOPK]
    b = pl.program_id(0)
    x = x_ref[...]
    q = x[:N_HEADS].astype(BF16)
    qf = q.astype(F32)
    tab = tab_ref[...]
    thr = _bucket_thresholds()
    n_cached = PAST_LEN // SEL_BLOCK

    def new_token(row_k, row_v):
        k_new = x[row_k:row_k + 1].astype(BF16).astype(F32)
        v_new = x[row_v:row_v + 1].astype(BF16).astype(F32)
        return jnp.sum(qf * k_new, axis=-1, keepdims=True) * ATTN_SCALE + tab[:, 0:1], v_new

    m, v_new = new_token(N_HEADS + 2, N_HEADS + 3)
    l = jnp.ones_like(m)
    acc = jnp.broadcast_to(v_new, (N_HEADS, HEAD_DIM))
    lane = lax.broadcasted_iota(jnp.int32, (1, SEL_BLOCK), 1)
    for k, blk in enumerate(blocks):
        sidx = info_ref[b, SEL_TOPK + k]
        dist = PAST_LEN - (sidx * SEL_BLOCK + lane)
        bias = jnp.broadcast_to(tab[:, 0:1], (N_HEADS, SEL_BLOCK))
        for bk in range(1, REL_BUCKETS):
            bias = jnp.where(dist >= thr[bk], tab[:, bk:bk + 1], bias)
        cached = jnp.full((N_HEADS, SEL_BLOCK), sidx, jnp.int32) < n_cached
        in_page = jnp.minimum(sidx, n_cached - 1) % (blk.shape[0] // SEL_BLOCK)
        rws = pl.ds(pl.multiple_of(in_page * SEL_BLOCK, SEL_BLOCK), SEL_BLOCK)
        m, l, acc = _flash_update(q, blk[rws, 2, :].astype(BF16), blk[rws, 3, :].astype(BF16), bias, cached,
                                  m, l, acc)
    o_sel = acc / l

    s_new, v_new = new_token(N_HEADS + 4, N_HEADS + 5)
    s = (lax.dot_general(q, win_ref[:, :HEAD_DIM].astype(BF16), _NT, preferred_element_type=F32) * ATTN_SCALE
         + biasw_ref[...])
    m = jnp.maximum(jnp.max(s, axis=-1, keepdims=True), s_new)
    p = jnp.exp(s - m)
    e_new = jnp.exp(s_new - m)
    l = jnp.sum(p, axis=-1, keepdims=True) + e_new
    o_win = (jnp.dot(p.astype(BF16), win_ref[:, HEAD_DIM:].astype(BF16), preferred_element_type=F32)
             + e_new * v_new) / l
    gates = jax.nn.sigmoid(gl_ref[...])
    o_ref[...] = (ocmp_ref[...] + gates[:, 1:2] * o_sel) + gates[:, 2:3] * o_win


def _nsa_sample(x_s, gl_s, cache_c_kv, state_c_win, page_table, rel_bias, w_cmp1, w_cmp2, w_cmp_pe):
    DB, n_pages = page_table.shape
    page = cache_c_kv.shape[1]
    L = n_pages * page
    assert L == PAST_LEN and state_c_win.shape[1] == WIN
    nch = L // CMP_STRIDE
    n_cmp = (L + 1 - CMP_BLOCK) // CMP_STRIDE + 1
    nsel_pad = 2 * LANE
    rows = x_s.shape[1]
    n = np.arange(nch)[:, None]
    jj = np.arange(nsel_pad)[None, :]
    overlap = ((n * CMP_STRIDE < jj * SEL_BLOCK + SEL_BLOCK) & (n * CMP_STRIDE + CMP_BLOCK > jj * SEL_BLOCK)
               & (n < n_cmp))
    ov = jnp.asarray(overlap, BF16)
    bias_c = _bias_table(rel_bias, PAST_LEN - (np.arange(nch) * CMP_STRIDE + CMP_BLOCK - 1))

    def page_spec(i):
        return pl.BlockSpec((None, page, 4, HEAD_DIM), lambda b, j, pt: (pt[b, j * PAGES_PER_STEP + i], 0, 0, 0))

    full = lambda shape: pl.BlockSpec(shape, lambda b, j, pt: (0,) * len(shape))
    ocmp, selm = pl.pallas_call(
        functools.partial(_nsa_sample_cmp_kernel, n_cmp=n_cmp),
        grid_spec=pltpu.PrefetchScalarGridSpec(
            num_scalar_prefetch=1,
            grid=(DB, n_pages // PAGES_PER_STEP),
            in_specs=[
                pl.BlockSpec((None, rows, HEAD_DIM), lambda b, j, pt: (b, 0, 0)),
                pl.BlockSpec((None, N_HEADS, 3), lambda b, j, pt: (b, 0, 0)),
                full((N_HEADS, nch)), full((nch, nsel_pad)), full(w_cmp1.shape), full(w_cmp2.shape),
                full(w_cmp_pe.shape),
            ] + [page_spec(i) for i in range(PAGES_PER_STEP)],
            out_specs=[pl.BlockSpec((None, N_HEADS, HEAD_DIM), lambda b, j, pt: (b, 0, 0)),
                       pl.BlockSpec((None, 8, nsel_pad), lambda b, j, pt: (b, 0, 0))],
            scratch_shapes=[pltpu.VMEM((2, L, HEAD_DIM), F32)],
        ),
        out_shape=[jax.ShapeDtypeStruct((DB, N_HEADS, HEAD_DIM), F32),
                   jax.ShapeDtypeStruct((DB, 8, nsel_pad), F32)],
        compiler_params=_compiler_params(2),
        name="nsa_sample_cmp",
    )(page_table, x_s, gl_s, bias_c, ov, w_cmp1.astype(BF16), w_cmp2, w_cmp_pe, *([cache_c_kv] * PAGES_PER_STEP))
    return _nsa_sample_sel(x_s, gl_s, ocmp, selm, cache_c_kv, state_c_win, page_table, rel_bias)


def _nsa_sample_sel(x_s, gl_s, ocmp, selm, cache_c_kv, state_c_win, page_table, rel_bias):
    DB, n_pages = page_table.shape
    page = cache_c_kv.shape[1]
    L = n_pages * page
    rows = x_s.shape[1]
    bias_w = _bias_table(rel_bias, WIN - np.arange(WIN))
    tab = rel_bias.T
    sidx = lax.top_k(selm[:, 0, :], SEL_TOPK)[1].astype(jnp.int32)
    n_cached = L // SEL_BLOCK
    halves = page // SEL_BLOCK

    pages = jnp.take_along_axis(page_table, jnp.minimum(sidx, n_cached - 1) // halves, axis=1)
    info = jnp.concatenate([pages, sidx], axis=1).astype(jnp.int32)

    def blk_spec(k):
        return pl.BlockSpec((None, page, 4, HEAD_DIM), lambda b, info: (info[b, k], 0, 0, 0))

    full2 = lambda shape: pl.BlockSpec(shape, lambda b, info: (0,) * len(shape))
    per_seq = lambda shape: pl.BlockSpec((None,) + shape, lambda b, info: (b,) + (0,) * len(shape))
    return pl.pallas_call(
        _nsa_sample_sel_kernel,
        grid_spec=pltpu.PrefetchScalarGridSpec(
            num_scalar_prefetch=1,
            grid=(DB,),
            in_specs=[per_seq((rows, HEAD_DIM)), per_seq((N_HEADS, 3)), per_seq((N_HEADS, HEAD_DIM)),
                      full2((N_HEADS, REL_BUCKETS)), full2((N_HEADS, WIN)), per_seq((WIN, 2 * HEAD_DIM))]
            + [blk_spec(k) for k in range(SEL_TOPK)],
            out_specs=per_seq((N_HEADS, HEAD_DIM)),
        ),
        out_shape=jax.ShapeDtypeStruct((DB, N_HEADS, HEAD_DIM), F32),
        compiler_params=_compiler_params(1),
        name="nsa_sample_sel",
    )(info, x_s, gl_s, ocmp, tab, bias_w, state_c_win.reshape(DB, WIN, 2 * HEAD_DIM), *([cache_c_kv] * SEL_TOPK))


def _mla_prompt_kernel(q_ref, lat_ref, wuv_ref, o_ref, acc_ref, *, tq):
    qi = pl.program_id(2)
    q = q_ref[...]

    def tile(t):
        return lat_ref[pl.ds(pl.multiple_of(t * tq, tq), tq), :]

    def scores(lat):
        return lax.dot_general(q, lat, _NT, preferred_element_type=F32) * MLA_SCALE

    lat = tile(qi)
    row = lax.broadcasted_iota(jnp.int32, (tq, tq), 0)
    col = lax.broadcasted_iota(jnp.int32, (tq, tq), 1)
    s = jnp.where(row >= col, scores(lat), NEG_INF)
    m = jnp.max(s, axis=-1, keepdims=True)
    p = jnp.exp(s - m)
    l = jnp.sum(p, axis=-1, keepdims=True)
    acc_ref[...] = jnp.dot(p.astype(BF16), lat[:, :MLA_KV_LORA], preferred_element_type=F32)

    def body(t, carry):
        m, l = carry
        lat = tile(t)
        s = scores(lat)
        m_new = jnp.maximum(m, jnp.max(s, axis=-1, keepdims=True))
        alpha = jnp.exp(m - m_new)
        p = jnp.exp(s - m_new)
        acc_ref[...] = alpha * acc_ref[...] + jnp.dot(p.astype(BF16), lat[:, :MLA_KV_LORA],
                                                      preferred_element_type=F32)
        return m_new, alpha * l + jnp.sum(p, axis=-1, keepdims=True)

    m, l = lax.fori_loop(0, qi, body, (m, l))
    o = (acc_ref[...] / l).astype(BF16)
    o_ref[...] = jnp.dot(o, wuv_ref[...].astype(BF16), preferred_element_type=F32).astype(o_ref.dtype)


def _mla_prompt(q_hm, lat_bf, w_kvb, B, S):
    tq = 256
    nq = S // tq
    width = MLA_KV_LORA + MLA_ROPE
    return pl.pallas_call(
        functools.partial(_mla_prompt_kernel, tq=tq),
        grid=(B, N_HEADS, nq),
        in_specs=[
            pl.BlockSpec((None, tq, width), lambda b, h, i: (h, b * nq + i, 0)),
            pl.BlockSpec((S, width), lambda b, h, i: (b, 0)),
            pl.BlockSpec((MLA_KV_LORA, MLA_V), lambda b, h, i: (0, 2 * h + 1)),
        ],
        out_specs=pl.BlockSpec((tq, MLA_V), lambda b, h, i: (b * nq + i, h)),
        out_shape=jax.ShapeDtypeStruct((B * S, N_HEADS * MLA_V), BF16),
        scratch_shapes=[pltpu.VMEM((tq, MLA_KV_LORA), F32)],
        compiler_params=_compiler_params(3),
        name="mla_prompt",
    )(q_hm, lat_bf, w_kvb)


def _mla_sample_kernel(pt_ref, q_ref, new_ref, *rest):
    pages = rest[:PAGES_PER_STEP]
    o_ref, m_scr, l_scr, acc_scr = rest[PAGES_PER_STEP:]
    j = pl.program_id(1)
    nsteps = pl.num_programs(1)
    q = q_ref[...].astype(BF16)
    lat_t = jnp.concatenate([p[...] for p in pages], axis=1).astype(BF16)
    s = jnp.dot(q, lat_t, preferred_element_type=F32) * MLA_SCALE
    m_blk = jnp.max(s, axis=-1, keepdims=True)

    @pl.when(j == 0)
    def _():
        m_scr[...] = jnp.full(m_scr.shape, NEG_INF, F32)
        l_scr[...] = jnp.zeros(l_scr.shape, F32)
        acc_scr[...] = jnp.zeros(acc_scr.shape, F32)

    m_old = m_scr[...]
    m_new = jnp.maximum(m_old, m_blk)
    alpha = jnp.exp(m_old - m_new)
    p = jnp.exp(s - m_new[:, 0:1])
    l_new = alpha * l_scr[...] + jnp.sum(p, axis=-1, keepdims=True)
    acc = alpha[:, 0:1] * acc_scr[...] + lax.dot_general(p.astype(BF16), lat_t[:MLA_KV_LORA], _NT,
                                                         preferred_element_type=F32)
    m_scr[...] = m_new
    l_scr[...] = l_new
    acc_scr[...] = acc

    @pl.when(j == nsteps - 1)
    def _():
        new = new_ref[...].astype(BF16).astype(F32)
        s_new = jnp.sum(q.astype(F32) * new, axis=-1, keepdims=True) * MLA_SCALE
        m_fin = jnp.maximum(m_new[:, 0:1], s_new)
        a = jnp.exp(m_new[:, 0:1] - m_fin)
        e = jnp.exp(s_new - m_fin)
        l_fin = a * l_new[:, 0:1] + e
        o_ref[...] = (a * acc + e * new[:, :MLA_KV_LORA]) / l_fin


def _mla_sample(q_s, lat_new, cache_d_latent, page_table):
    DB, n_pages = page_table.shape
    page, width = cache_d_latent.shape[1:]
    cache_t = jnp.swapaxes(cache_d_latent, 1, 2)

    def page_spec(i):
        return pl.BlockSpec((None, width, page), lambda b, j, pt: (pt[b, j * PAGES_PER_STEP + i], 0, 0))

    grid_spec = pltpu.PrefetchScalarGridSpec(
        num_scalar_prefetch=1,
        grid=(DB, n_pages // PAGES_PER_STEP),
        in_specs=[
            pl.BlockSpec((None, N_HEADS, width), lambda b, j, pt: (b, 0, 0)),
            pl.BlockSpec((None, 1, width), lambda b, j, pt: (b, 0, 0)),
        ] + [page_spec(i) for i in range(PAGES_PER_STEP)],
        out_specs=pl.BlockSpec((None, N_HEADS, MLA_KV_LORA), lambda b, j, pt: (b, 0, 0)),
        scratch_shapes=[pltpu.VMEM((N_HEADS, HEAD_DIM), F32), pltpu.VMEM((N_HEADS, HEAD_DIM), F32),
                        pltpu.VMEM((N_HEADS, MLA_KV_LORA), F32)],
    )
    return pl.pallas_call(
        _mla_sample_kernel,
        grid_spec=grid_spec,
        out_shape=jax.ShapeDtypeStruct((DB, N_HEADS, MLA_KV_LORA), F32),
        compiler_params=_compiler_params(2),
        name="mla_sample",
    )(page_table, q_s, lat_new, *([cache_t] * PAGES_PER_STEP))


def _rel_bucket(dist):
    n = jnp.maximum(dist, 0)
    max_exact = REL_BUCKETS // 2
    large = max_exact + (jnp.log(jnp.maximum(n, 1).astype(F32) / max_exact)
                         / math.log(REL_MAX_DIST / max_exact) * (REL_BUCKETS - max_exact)).astype(jnp.int32)
    large = jnp.minimum(large, REL_BUCKETS - 1)
    return jnp.where(n < max_exact, n, large)


def _softmax_lse(logits, mask, axis):
    l = jnp.where(mask, logits.astype(F32), NEG_INF)
    m = jnp.max(l, axis=axis, keepdims=True)
    e = jnp.where(mask, jnp.exp(l - m), 0.0)
    s = jnp.sum(e, axis=axis, keepdims=True)
    s = jnp.maximum(s, 1e-30)
    return e / s, jnp.squeeze(m + jnp.log(s), axis)


def _rope(x, pos):
    half = x.shape[-1] // 2
    inv = ROPE_THETA ** (-jnp.arange(half, dtype=F32) / half)
    ang = pos.astype(F32)[:, None] * inv[None, :]
    cos = jnp.cos(ang)[:, None, :]
    sin = jnp.sin(ang)[:, None, :]
    xf = x.astype(F32)
    x1, x2 = xf[..., :half], xf[..., half:]
    return jnp.concatenate([x1 * cos - x2 * sin, x1 * sin + x2 * cos], -1).astype(x.dtype)


def _rmsnorm_jax(x, g):
    xf = x.astype(F32)
    y = xf * lax.rsqrt(jnp.mean(xf * xf, axis=-1, keepdims=True) + RMS_EPS)
    return (y * g.astype(F32)).astype(x.dtype)


def _gather_pages(pool, pt_row):
    g = pool[pt_row]
    return g.reshape((-1,) + pool.shape[2:])


def _map_query_blocks(fn, q_arrays, q_pos, block):
    T = q_pos.shape[0]
    if T <= block or T % block:
        return fn(*q_arrays, q_pos)
    nb = T // block
    xs = tuple(a.reshape((nb, block) + a.shape[1:]) for a in q_arrays) + (q_pos.reshape(nb, block),)
    out = lax.map(lambda a: fn(*a), xs)
    return out.reshape((T,) + out.shape[2:])


def _moba_seq(q, kv, q_pos, rel_bias):
    L = kv.shape[0]
    nb = -(-L // MOBA_BLOCK)
    kv = jnp.pad(kv, ((0, nb * MOBA_BLOCK - L), (0, 0), (0, 0), (0, 0)))
    kvb = kv.reshape(nb, MOBA_BLOCK, 2, MOBA_KV_HEADS, HEAD_DIM).transpose(2, 3, 0, 1, 4)
    kb, vb = kvb[0], kvb[1]
    kmean = jnp.mean(kb.astype(F32), axis=2)
    topk = min(MOBA_TOPK, nb)
    G = N_HEADS // MOBA_KV_HEADS
    tb = rel_bias.T.reshape(MOBA_KV_HEADS, G, REL_BUCKETS)
    kv_i = jnp.arange(MOBA_KV_HEADS)[None, :, None, None]
    g_i = jnp.arange(G)[None, None, :, None, None]

    def block_fn(qb, pos):
        Tb = qb.shape[0]
        qg = qb.reshape(Tb, MOBA_KV_HEADS, G, HEAD_DIM)
        own = pos // MOBA_BLOCK
        gate = jnp.einsum('tkgd,knd->tkgn', qg.astype(F32), kmean)
        past = jnp.arange(nb)[None, :] < own[:, None]
        gate = jnp.where(past[:, None, None, :], gate, NEG_INF)
        _, sel = lax.top_k(gate, topk)
        sel_ok = sel < own[:, None, None, None]
        own_b = jnp.broadcast_to(own[:, None, None, None], (Tb, MOBA_KV_HEADS, G, 1))
        blocks = jnp.concatenate([sel, own_b], -1)
        ok_blk = jnp.concatenate([sel_ok, jnp.ones_like(own_b, dtype=bool)], -1)
        kg = kb[kv_i, blocks]
        vg = vb[kv_i, blocks]
        kpos = blocks[..., None] * MOBA_BLOCK + jnp.arange(MOBA_BLOCK)
        dist = pos[:, None, None, None, None] - kpos
        logits = (jnp.einsum('tkgd,tkgnsd->tkgns', qg, kg).astype(F32) * ATTN_SCALE
                  + tb[kv_i[..., None], g_i, _rel_bucket(dist)])
        mask = ok_blk[..., None] & (dist >= 0)
        nk = (topk + 1) * MOBA_BLOCK
        p, _ = _softmax_lse(logits.reshape(Tb, MOBA_KV_HEADS, G, nk), mask.reshape(Tb, MOBA_KV_HEADS, G, nk), -1)
        out = jnp.einsum('tkgj,tkgjd->tkgd', p.astype(vg.dtype), vg.reshape(Tb, MOBA_KV_HEADS, G, nk, HEAD_DIM))
        return out.reshape(Tb, N_HEADS, HEAD_DIM)

    return _map_query_blocks(block_fn, (q,), q_pos, GATHER_Q_BLOCK)


def _moba_core(qp, kvp, qs, kvs, cache_a_kv, page_table, pos_p, pos_s, rel_bias):
    op = lax.map(lambda a: _moba_seq(a[0], a[1], pos_p, rel_bias), (qp, kvp))

    def sample_seq(a):
        q, kv_new, pt = a
        kv_all = jnp.concatenate([_gather_pages(cache_a_kv, pt), kv_new], axis=0)
        return _moba_seq(q, kv_all, pos_s, rel_bias)
    os_ = lax.map(sample_seq, (qs, kvs, page_table))
    return op, os_


def _dilated_prompt(q, kv, win, dil, rel_bias):
    B, S = q.shape[:2]
    R = win // dil
    Sp = -(-S // win) * win
    nbu = Sp // win
    G = N_HEADS // DIL_KV_HEADS
    q = jnp.pad(q, ((0, 0), (0, Sp - S), (0, 0), (0, 0)))
    kv = jnp.pad(kv, ((0, 0), (0, Sp - S), (0, 0), (0, 0), (0, 0)))
    qr = q.reshape(B, nbu, R, dil, DIL_KV_HEADS, G, HEAD_DIM)
    kvr = kv.reshape(B, nbu, R, dil, 2, DIL_KV_HEADS, HEAD_DIM)
    prev = jnp.pad(kvr, ((0, 0), (1, 0), (0, 0), (0, 0), (0, 0), (0, 0), (0, 0)))[:, :nbu]
    kk = jnp.concatenate([prev, kvr], axis=2)
    a = jnp.arange(R)[:, None]
    j = jnp.arange(2 * R)[None, :]
    steps = R + a - j
    band = (steps >= 0) & (steps <= R)
    kvalid = (jnp.arange(nbu)[:, None] > 0) | (jnp.arange(2 * R)[None, :] >= R)
    ok = band[None] & kvalid[:, None, :]
    bias = rel_bias[_rel_bucket(steps * dil)].transpose(2, 0, 1).reshape(DIL_KV_HEADS, G, R, 2 * R)
    logits = jnp.einsum('bnqrkgd,bnjrkd->bnrkgqj', qr, kk[:, :, :, :, 0]).astype(F32) * ATTN_SCALE + bias
    p, lse = _softmax_lse(logits, ok[None, :, None, None, None, :, :], -1)
    out = jnp.einsum('bnrkgqj,bnjrkd->bnqrkgd', p.astype(kk.dtype), kk[:, :, :, :, 1])
    out = out.reshape(B, Sp, N_HEADS, HEAD_DIM)[:, :S]
    lse = lse.transpose(0, 1, 5, 2, 3, 4).reshape(B, Sp, N_HEADS)[:, :S]
    return out, lse


def _dilated_sample_jax(q, kv_new, buf, win, dil, rel_bias):
    DB, T = q.shape[:2]
    Lbuf = buf.shape[1]
    R = win // dil
    G = N_HEADS // DIL_KV_HEADS
    ext = jnp.concatenate([buf, kv_new], axis=1)
    m = jnp.arange(R + 1)
    idx = Lbuf + jnp.arange(T)[:, None] - m[None, :] * dil
    ok = idx >= 0
    g = ext[:, jnp.clip(idx, 0)]
    qg = q.reshape(DB, T, DIL_KV_HEADS, G, HEAD_DIM)
    bias = rel_bias[_rel_bucket(m * dil)].T.reshape(DIL_KV_HEADS, G, R + 1)
    logits = jnp.einsum('btkgd,btjkd->btkgj', qg, g[:, :, :, 0]).astype(F32) * ATTN_SCALE + bias
    p, lse = _softmax_lse(logits, ok[None, :, None, None, :], -1)
    out = jnp.einsum('btkgj,btjkd->btkgd', p.astype(g.dtype), g[:, :, :, 1])
    return out.reshape(DB, T, N_HEADS, HEAD_DIM), lse.reshape(DB, T, N_HEADS), ext[:, T:]


def _merge_groups(outs, lses):
    w = jax.nn.softmax(jnp.stack(lses, 0), axis=0)
    o = jnp.sum(w[..., None] * jnp.stack(outs, 0).astype(F32), axis=0)
    B, T = o.shape[:2]
    return o.reshape(B, T, D_ATTN).astype(outs[0].dtype)


def _dilated_core(qp, kvp, qs, kvs, bufs, rel_bias):
    S = qp.shape[1]
    outs_p, lses_p, outs_s, lses_s, new_p, new_s = [], [], [], [], [], []
    for gi, (win, dil) in enumerate(DIL_PATTERNS):
        o, l = _dilated_prompt(qp[:, :, gi], kvp[:, :, gi], win, dil, rel_bias)
        outs_p.append(o)
        lses_p.append(l)
        new_p.append(kvp[:, S - min(win, S):, gi])
        o, l, nbuf = _dilated_sample(qs[:, :, gi], kvs[:, :, gi], bufs[gi], win, dil, rel_bias)
        outs_s.append(o)
        lses_s.append(l)
        new_s.append(nbuf)
    return _merge_groups(outs_p, lses_p), _merge_groups(outs_s, lses_s), new_p, new_s


def _compress(kv2, w1, w2, pe):
    L = kv2.shape[0]
    n_cmp = (L - CMP_BLOCK) // CMP_STRIDE + 1
    idx = jnp.arange(n_cmp)[:, None] * CMP_STRIDE + jnp.arange(CMP_BLOCK)[None, :]
    blk = kv2[idx].transpose(2, 0, 1, 3) + pe[:, None]
    hid = jax.nn.gelu(jnp.einsum('cnx,cxh->cnh', blk.reshape(2, n_cmp, CMP_BLOCK * HEAD_DIM), w1))
    return jnp.einsum('cnh,chd->cnd', hid, w2)


def _nsa_seq(q, gates, kv4, q_pos, rel_bias, w_cmp1, w_cmp2, w_cmp_pe):
    L = kv4.shape[0]
    cmp = _compress(kv4[:, 0:2], w_cmp1, w_cmp2, w_cmp_pe)
    kc, vc = cmp[0], cmp[1]
    n_cmp = kc.shape[0]
    tok_start = jnp.arange(n_cmp) * CMP_STRIDE
    cmp_end = tok_start + CMP_BLOCK - 1
    n_sel = -(-L // SEL_BLOCK)
    sel_kv = jnp.pad(kv4[:, 2:4], ((0, n_sel * SEL_BLOCK - L), (0, 0), (0, 0))).reshape(n_sel, SEL_BLOCK, 2, HEAD_DIM)
    blk_start = jnp.arange(n_sel) * SEL_BLOCK
    overlap = ((tok_start[:, None] < blk_start[None, :] + SEL_BLOCK)
               & (tok_start[:, None] + CMP_BLOCK > blk_start[None, :])).astype(F32)
    n_top = min(SEL_TOPK, n_sel)

    def block_fn(qb, gb, pos):
        Tb = qb.shape[0]
        dc = pos[:, None] - cmp_end[None, :]
        lc = (jnp.einsum('thd,nd->thn', qb, kc).astype(F32) * ATTN_SCALE
              + rel_bias[_rel_bucket(dc)].transpose(0, 2, 1))
        pc, _ = _softmax_lse(lc, (dc >= 0)[:, None, :], -1)
        o_cmp = jnp.einsum('thn,nd->thd', pc.astype(vc.dtype), vc)
        imp = jnp.sum(pc, axis=1) @ overlap
        cur = pos // SEL_BLOCK
        j = jnp.arange(n_sel)[None, :]
        forced = (j == 0) | (j == cur[:, None]) | (j == cur[:, None] - 1)
        score = jnp.where(j > cur[:, None], NEG_INF, jnp.where(forced, FORCE_SCORE, imp))
        _, sidx = lax.top_k(score, n_top)
        g = sel_kv[sidx]
        kpos = sidx[..., None] * SEL_BLOCK + jnp.arange(SEL_BLOCK)
        ds = pos[:, None, None] - kpos
        ls = (jnp.einsum('thd,tnsd->thns', qb, g[..., 0, :]).astype(F32) * ATTN_SCALE
              + rel_bias[_rel_bucket(ds)].transpose(0, 3, 1, 2))
        nk = n_top * SEL_BLOCK
        ps, _ = _softmax_lse(ls.reshape(Tb, N_HEADS, nk), (ds >= 0).reshape(Tb, 1, nk), -1)
        o_sel = jnp.einsum('thj,tjd->thd', ps.astype(g.dtype), g[..., 1, :].reshape(Tb, nk, HEAD_DIM))
        return gb[..., 0:1] * o_cmp + gb[..., 1:2] * o_sel

    return _map_query_blocks(block_fn, (q, gates), q_pos, GATHER_Q_BLOCK)


def _window_prompt(q, kvw, rel_bias):
    B, S = q.shape[:2]
    P = WIN // WIN_BLOCK
    nb = S // WIN_BLOCK
    J = (P + 1) * WIN_BLOCK
    qr = q.reshape(B, nb, WIN_BLOCK, N_HEADS, HEAD_DIM)
    kvr = jnp.pad(kvw.reshape(B, nb, WIN_BLOCK, 2, HEAD_DIM), ((0, 0), (P, 0), (0, 0), (0, 0), (0, 0)))
    kk = jnp.concatenate([kvr[:, i:i + nb] for i in range(P + 1)], axis=2)
    dist = P * WIN_BLOCK + jnp.arange(WIN_BLOCK)[:, None] - jnp.arange(J)[None, :]
    kpos = jnp.arange(nb)[:, None] * WIN_BLOCK - P * WIN_BLOCK + jnp.arange(J)[None, :]
    ok = ((dist >= 0) & (dist <= WIN))[None] & (kpos >= 0)[:, None, :]
    bias = rel_bias[_rel_bucket(dist)].transpose(2, 0, 1)
    logits = jnp.einsum('bnqhd,bnjd->bnhqj', qr, kk[:, :, :, 0]).astype(F32) * ATTN_SCALE + bias
    p, _ = _softmax_lse(logits, ok[None, :, None], -1)
    out = jnp.einsum('bnhqj,bnjd->bnqhd', p.astype(kk.dtype), kk[:, :, :, 1])
    return out.reshape(B, S, N_HEADS, HEAD_DIM)


def _window_sample(q, kvw_new, buf, q_pos, rel_bias):
    T = q.shape[1]
    Lbuf = buf.shape[1]
    ext = jnp.concatenate([buf, kvw_new], axis=1)
    kpos = PAST_LEN - Lbuf + jnp.arange(Lbuf + T)
    dist = q_pos[:, None] - kpos[None, :]
    ok = (dist >= 0) & (dist <= WIN)
    bias = rel_bias[_rel_bucket(dist)].transpose(2, 0, 1)
    logits = jnp.einsum('bthd,bjd->bhtj', q, ext[:, :, 0]).astype(F32) * ATTN_SCALE + bias
    p, _ = _softmax_lse(logits, ok[None, None], -1)
    out = jnp.einsum('bhtj,bjd->bthd', p.astype(ext.dtype), ext[:, :, 1])
    return out, ext[:, T:]


def _nsa_sample_core(qs, kvs, gs, cache_c_kv, state_c_win, page_table, pos_s, rel_bias, w_cmp1, w_cmp2, w_cmp_pe):
    def sample_seq(a):
        q, g, kv_new, pt = a
        kv_all = jnp.concatenate([_gather_pages(cache_c_kv, pt), kv_new[:, :4]], axis=0)
        return _nsa_seq(q, g, kv_all, pos_s, rel_bias, w_cmp1, w_cmp2, w_cmp_pe)
    os_ = lax.map(sample_seq, (qs, gs, kvs, page_table))
    ow, win_s = _window_sample(qs, kvs[:, :, 4:6], state_c_win, pos_s, rel_bias)
    os_ = os_ + gs[..., 2:3] * ow
    return os_, win_s


def _mla_seq(q, lat, q_pos):
    kpos = jnp.arange(lat.shape[0])
    c = lat[:, :MLA_KV_LORA]

    def block_fn(qb, pos):
        logits = jnp.einsum('thc,lc->htl', qb, lat).astype(F32) * MLA_SCALE
        p, _ = _softmax_lse(logits, (kpos[None, :] <= pos[:, None])[None], -1)
        return jnp.einsum('htl,lc->thc', p.astype(c.dtype), c)

    return _map_query_blocks(block_fn, (q,), q_pos, DENSE_Q_BLOCK)


def kernel(x_prompt, x_sample, cache_a_kv, state_b_kv1, state_b_kv2, state_b_kv3, cache_c_kv, state_c_win,
           cache_d_latent, page_table, rel_bias, g_attn_norm, g_ffn_norm, g_final_norm,
           w_a_qkv, w_a_o, w_b_qkv, w_b_o, w_c_qkv, w_c_gate, w_c_cmp1, w_c_cmp2, w_c_cmp_pe, w_c_o,
           w_d_qa, g_d_qnorm, w_d_qb, w_d_kva, g_d_kvnorm, w_d_kvb, w_d_o, w_ffn_in, w_ffn_out):
    B, S, D = x_prompt.shape
    DB, T, _ = x_sample.shape
    NP = B * S
    NS = DB * T
    N = NP + NS
    TM = 640
    assert N % TM == 0
    pos_p = jnp.arange(S, dtype=jnp.int32)
    pos_s = PAST_LEN + jnp.arange(T, dtype=jnp.int32)

    x = jnp.concatenate([x_prompt.reshape(NP, D), x_sample.reshape(NS, D)], axis=0)
    w_ffn_out_bf = w_ffn_out.astype(BF16)

    def split(a):
        return a[:NP].reshape((B, S) + a.shape[1:]), a[NP:].reshape((DB, T) + a.shape[1:])

    outs = {}
    for layer in range(DEPTH):
        h = _rmsnorm(x, g_attn_norm[layer], BF16, TM)
        kind = layer % 4
        if kind == 0:
            qkv, qkv_bf = _matmul(h, w_a_qkv, tm=TM, tn=512, bf_copy=True)
            kv = qkv[:, D_ATTN:].reshape(N, 2, MOBA_KV_HEADS, HEAD_DIM)
            kvp, kvs = split(kv)
            op = _moba_prompt(qkv, qkv_bf, rel_bias, B, S)
            x_s = qkv[NP:].reshape(DB, N_HEADS + 2 * MOBA_KV_HEADS, HEAD_DIM)
            os_ = _moba_sample(x_s, cache_a_kv, page_table, rel_bias)
            o = jnp.concatenate([op, os_.reshape(NS, D_ATTN).astype(BF16)], axis=0)
            x = _matmul(o, w_a_o, tm=TM, tn=512, res=x)
            outs['a_kv'] = (kvp, kvs)
        elif kind == 1:
            n_g = len(DIL_PATTERNS)
            width = D_ATTN + 2 * DIL_KV_HEADS * HEAD_DIM
            qkv, qkv_bf = _matmul(h, w_b_qkv, tm=TM, tn=512, bf_copy=True)
            bufs = (state_b_kv1, state_b_kv2, state_b_kv3)
            kv = qkv.reshape(N, n_g, width)[..., D_ATTN:].reshape(N, n_g, 2, DIL_KV_HEADS, HEAD_DIM)
            kvp, kvs = split(kv)
            op = _dilated_merge([_dilated_prompt_group(qkv_bf, rel_bias, gi, B, S) for gi in range(n_g)], 256)
            x_s = qkv[NP:].reshape(DB, n_g * (N_HEADS + 2 * DIL_KV_HEADS), HEAD_DIM)
            os_ = _dilated_sample(x_s, bufs, rel_bias)
            new_p = [kvp[:, S - min(win, S):, gi] for gi, (win, _) in enumerate(DIL_PATTERNS)]
            new_s = [jnp.concatenate([buf, kvs[:, :, gi]], axis=1)[:, T:] for gi, buf in enumerate(bufs)]
            o = jnp.concatenate([op, os_.reshape(NS, D_ATTN).astype(BF16)], axis=0)
            x = _matmul(o, w_b_o, tm=TM, tn=512, res=x)
            outs['b'] = (new_p, new_s)
        elif kind == 2:
            qkv, qkv_bf = _matmul(h, w_c_qkv, tm=TM, tn=256, bf_copy=True)
            gl = _matmul(h, w_c_gate, tm=TM, tn=3 * N_HEADS)
            kv = qkv[:, D_ATTN:].reshape(N, NSA_KV_ROWS, HEAD_DIM)
            kvp, kvs = split(kv)
            op = _nsa_prompt(qkv, qkv_bf, gl, rel_bias, w_c_cmp1, w_c_cmp2, w_c_cmp_pe, B, S)
            x_s = qkv[NP:].reshape(DB, N_HEADS + NSA_KV_ROWS, HEAD_DIM)
            os_ = _nsa_sample(x_s, gl[NP:].reshape(DB, N_HEADS, 3), cache_c_kv, state_c_win, page_table, rel_bias,
                              w_c_cmp1, w_c_cmp2, w_c_cmp_pe)
            win_s = jnp.concatenate([state_c_win, kvs[:, :, 4:6]], axis=1)[:, T:]
            o = jnp.concatenate([op, os_.reshape(NS, D_ATTN).astype(BF16)], axis=0)
            x = _matmul(o, w_c_o, tm=TM, tn=512, res=x)
            outs['c'] = (kvp[:, :, :4], kvs[:, :, :4], kvp[:, S - min(WIN, S):, 4:6], win_s)
        else:
            w_kvb_r = w_d_kvb.reshape(MLA_KV_LORA, N_HEADS, MLA_NOPE + MLA_V)
            w_uk = w_kvb_r[..., :MLA_NOPE]
            w_uv = w_kvb_r[..., MLA_NOPE:]
            qa = _matmul(h, w_d_qa, tm=TM, tn=512)
            qn = _rmsnorm(qa, g_d_qnorm, BF16, TM)
            qfull = _matmul(qn, w_d_qb, tm=TM, tn=512).reshape(N, N_HEADS, MLA_NOPE + MLA_ROPE)
            kva = _matmul(h, w_d_kva, tm=TM, tn=MLA_KV_LORA + MLA_ROPE)
            pos_all = jnp.concatenate([jnp.tile(pos_p, B), jnp.tile(pos_s, DB)])

            def rope_rows(xr):
                half = xr.shape[-1] // 2
                inv = ROPE_THETA ** (-jnp.arange(half, dtype=F32) / half)
                ang = pos_all.astype(F32)[:, None] * inv[None, :]
                cos = jnp.cos(ang)[:, None, :]
                sin = jnp.sin(ang)[:, None, :]
                x1, x2 = xr[..., :half], xr[..., half:]
                return jnp.concatenate([x1 * cos - x2 * sin, x1 * sin + x2 * cos], -1)

            q_pe = rope_rows(qfull[..., MLA_NOPE:])
            q_lat = jnp.einsum('thn,chn->thc', qfull[..., :MLA_NOPE], w_uk)
            c = _rmsnorm_jax(kva[:, :MLA_KV_LORA], g_d_kvnorm)
            k_pe = rope_rows(kva[:, None, MLA_KV_LORA:])[:, 0, :]
            qcat = jnp.concatenate([q_lat, q_pe], -1)
            lat = jnp.concatenate([c, k_pe], -1)
            latp, lats = split(lat)
            q_hm = jnp.transpose(qcat[:NP], (1, 0, 2)).astype(BF16)
            op = _mla_prompt(q_hm, lat.astype(BF16), w_d_kvb, B, S)
            os_lat = _mla_sample(qcat[NP:], lat[NP:].reshape(DB, 1, MLA_KV_LORA + MLA_ROPE), cache_d_latent,
                                 page_table)
            os_ = jnp.einsum('thc,chv->thv', os_lat, w_uv).reshape(NS, N_HEADS * MLA_V)
            o = jnp.concatenate([op, os_.astype(BF16)], axis=0)
            x = _matmul(o, w_d_o, tm=TM, tn=512, res=x)
            outs['d'] = (latp, lats)

        h2 = _rmsnorm(x, g_ffn_norm[layer], BF16, TM)
        act = _ffn_in(h2, w_ffn_in, layer, tm=TM, tf=512)
        x = _matmul(act, w_ffn_out_bf, tm=TM, tn=512, res=x, w_layer=layer)

    y = _rmsnorm(x, g_final_norm, F32, TM)
    y_prompt = y[:NP].reshape(B, S, D)
    y_sample = y[NP:].reshape(DB, T, D)
    a_kv_prompt, a_kv_sample = outs['a_kv']
    new_p, new_s = outs['b']
    c_kv_prompt, c_kv_sample, c_win_prompt, c_win_sample = outs['c']
    d_latent_prompt, d_latent_sample = outs['d']
    return (y_prompt, y_sample, a_kv_prompt, a_kv_sample, new_p[0], new_s[0], new_p[1], new_s[1],
            new_p[2], new_s[2], c_kv_prompt, c_kv_sample, c_win_prompt, c_win_sample,
            d_latent_prompt, d_latent_sample)
```

```python
import functools
import math

import jax
import jax.numpy as jnp
import numpy as np
from jax import lax
from jax.experimental import pallas as pl
from jax.experimental.pallas import tpu as pltpu

F32 = jnp.float32
BF16 = jnp.bfloat16

D_MODEL = 2048
DEPTH = 4
PAST_LEN = 8192
HEAD_DIM = 128
N_HEADS = 16
D_ATTN = N_HEADS * HEAD_DIM
ATTN_SCALE = HEAD_DIM ** -0.5
D_FF = 5632
RMS_EPS = 1e-6
REL_BUCKETS = 32
REL_MAX_DIST = 128
NEG_INF = -1e30
FORCE_SCORE = 1e30
GATHER_Q_BLOCK = 16
DENSE_Q_BLOCK = 128
MOBA_BLOCK = 256
MOBA_TOPK = 3
MOBA_KV_HEADS = 2
DIL_PATTERNS = ((128, 1), (512, 4), (2048, 16))
DIL_KV_HEADS = 4
CMP_BLOCK = 32
CMP_STRIDE = 16
SEL_BLOCK = 64
SEL_TOPK = 16
WIN = 512
WIN_BLOCK = 128
NSA_KV_ROWS = 6
MLA_Q_LORA = 512
MLA_KV_LORA = 512
MLA_NOPE = 128
MLA_ROPE = 64
MLA_V = 128
MLA_SCALE = (MLA_NOPE + MLA_ROPE) ** -0.5
ROPE_THETA = 10000.0

VMEM_LIMIT_BYTES = 56 * 1024 * 1024
LANE = 128


def _compiler_params(n_grid_axes):
    return pltpu.CompilerParams(
        dimension_semantics=("arbitrary",) * n_grid_axes,
        vmem_limit_bytes=VMEM_LIMIT_BYTES,
    )


def _rmsnorm_kernel(x_ref, g_ref, o_ref):
    x = x_ref[...]
    ms = jnp.mean(x * x, axis=-1, keepdims=True)
    o_ref[...] = ((x * lax.rsqrt(ms + RMS_EPS)) * g_ref[...]).astype(o_ref.dtype)


def _rmsnorm(x, g, out_dtype, tm):
    n, d = x.shape
    return pl.pallas_call(
        _rmsnorm_kernel,
        grid=(n // tm,),
        in_specs=[pl.BlockSpec((tm, d), lambda i: (i, 0)), pl.BlockSpec((1, d), lambda i: (0, 0))],
        out_specs=pl.BlockSpec((tm, d), lambda i: (i, 0)),
        out_shape=jax.ShapeDtypeStruct((n, d), out_dtype),
        compiler_params=_compiler_params(1),
        name="rmsnorm",
    )(x, g.reshape(1, d))


def _mm_kernel(*refs, has_res, cast_w, bf_copy):
    x_ref, w_ref = refs[0], refs[1]
    rest = refs[2:]
    if has_res:
        r_ref, rest = rest[0], rest[1:]
    o_ref = rest[0]
    rest = rest[1:]
    if bf_copy:
        ob_ref, rest = rest[0], rest[1:]
    if cast_w:
        wb_ref = rest[0]

        @pl.when(pl.program_id(1) == 0)
        def _():
            wb_ref[...] = w_ref[...].astype(BF16)

        w = wb_ref[...]
    else:
        w = w_ref[...]
    acc = jnp.dot(x_ref[...].astype(BF16), w, preferred_element_type=F32)
    if has_res:
        acc = r_ref[...] + acc
    o_ref[...] = acc.astype(o_ref.dtype)
    if bf_copy:
        ob_ref[...] = acc.astype(BF16)


def _matmul(x, w, *, tm, tn, res=None, out_dtype=F32, w_layer=None, w_col0=0, n_out=None, bf_copy=False):
    m, k = x.shape
    n = n_out if n_out is not None else w.shape[-1]
    assert m % tm == 0 and n % tn == 0 and w_col0 % tn == 0
    cast_w = w.dtype != BF16
    cb = w_col0 // tn
    if w_layer is None:
        w_spec = pl.BlockSpec((k, tn), lambda j, i: (0, j + cb))
    else:
        w_spec = pl.BlockSpec((None, k, tn), lambda j, i: (w_layer, 0, j + cb))
    in_specs = [pl.BlockSpec((tm, k), lambda j, i: (i, 0)), w_spec]
    args = [x, w]
    if res is not None:
        in_specs.append(pl.BlockSpec((tm, tn), lambda j, i: (i, j)))
        args.append(res)
    out_spec = pl.BlockSpec((tm, tn), lambda j, i: (i, j))
    out_shape = jax.ShapeDtypeStruct((m, n), out_dtype)
    return pl.pallas_call(
        functools.partial(_mm_kernel, has_res=res is not None, cast_w=cast_w, bf_copy=bf_copy),
        grid=(n // tn, m // tm),
        in_specs=in_specs,
        out_specs=[out_spec, out_spec] if bf_copy else out_spec,
        out_shape=[out_shape, jax.ShapeDtypeStruct((m, n), BF16)] if bf_copy else out_shape,
        scratch_shapes=[pltpu.VMEM((k, tn), BF16)] if cast_w else [],
        compiler_params=_compiler_params(2),
        name="matmul",
    )(*args)


def _ffn_in_kernel(x_ref, wg_ref, wu_ref, o_ref, wgb_ref, wub_ref):
    @pl.when(pl.program_id(1) == 0)
    def _():
        wgb_ref[...] = wg_ref[...].astype(BF16)
        wub_ref[...] = wu_ref[...].astype(BF16)

    x = x_ref[...]
    g = jnp.dot(x, wgb_ref[...], preferred_element_type=F32)
    u = jnp.dot(x, wub_ref[...], preferred_element_type=F32)
    o_ref[...] = ((g * jax.nn.sigmoid(g)) * u).astype(o_ref.dtype)


def _ffn_in(h, w_ffn_in, layer, *, tm, tf):
    m, k = h.shape
    nf = D_FF // tf
    return pl.pallas_call(
        _ffn_in_kernel,
        grid=(nf, m // tm),
        in_specs=[
            pl.BlockSpec((tm, k), lambda j, i: (i, 0)),
            pl.BlockSpec((None, k, tf), lambda j, i: (layer, 0, j)),
            pl.BlockSpec((None, k, tf), lambda j, i: (layer, 0, j + nf)),
        ],
        out_specs=pl.BlockSpec((tm, tf), lambda j, i: (i, j)),
        out_shape=jax.ShapeDtypeStruct((m, D_FF), BF16),
        scratch_shapes=[pltpu.VMEM((k, tf), BF16), pltpu.VMEM((k, tf), BF16)],
        compiler_params=_compiler_params(2),
        name="ffn_in",
    )(h, w_ffn_in, w_ffn_in)


def _np_bucket(dist):
    n = np.maximum(dist, 0)
    max_exact = REL_BUCKETS // 2
    large = max_exact + (np.log(np.maximum(n, 1).astype(np.float32) / np.float32(max_exact))
                         / np.float32(math.log(REL_MAX_DIST / max_exact))
                         * np.float32(REL_BUCKETS - max_exact)).astype(np.int32)
    large = np.minimum(large, REL_BUCKETS - 1)
    return np.where(n < max_exact, n, large).astype(np.int32)


def _bias_table(rel_bias, dist):
    b = rel_bias[jnp.asarray(_np_bucket(dist))]
    return jnp.moveaxis(b, -1, 0)


_NT = (((1,), (1,)), ((), ()))


def _moba_prompt_kernel(q_ref, kbf_ref, vbf_ref, kf_ref, d0_ref, d1_ref, bfar_ref, o_ref, kmean_ref, *, nb, group):
    h = pl.program_id(1)
    qi = pl.program_id(2)
    blk = MOBA_BLOCK

    @pl.when((qi == 0) & (h % group == 0))
    def _():
        kf = kf_ref[...]
        kmean_ref[...] = jnp.mean(kf.reshape(nb, blk, HEAD_DIM), axis=1).astype(BF16)

    q = q_ref[...]
    gate = lax.dot_general(q, kmean_ref[...], _NT, preferred_element_type=F32)
    col = lax.broadcasted_iota(jnp.int32, gate.shape, 1)
    colf = col.astype(F32)
    g = jnp.where(col < qi, gate, NEG_INF)
    sel = jnp.zeros(gate.shape, F32)
    for _ in range(MOBA_TOPK):
        mx = jnp.max(g, axis=-1, keepdims=True)
        idx = jnp.min(jnp.where(g == mx, colf, float(nb)), axis=-1, keepdims=True)
        pick = colf == idx
        sel = jnp.where(pick, 1.0, sel)
        g = jnp.where(pick, -jnp.inf, g)
    sel = jnp.where(col < qi, sel, 0.0)

    def logits(n):
        start = pl.multiple_of(n * blk, blk)
        k = kbf_ref[pl.ds(start, blk), :]
        return lax.dot_general(q, k, _NT, preferred_element_type=F32) * ATTN_SCALE

    def values(n):
        start = pl.multiple_of(n * blk, blk)
        return vbf_ref[pl.ds(start, blk), :]

    row = lax.broadcasted_iota(jnp.int32, (blk, blk), 0)
    colk = lax.broadcasted_iota(jnp.int32, (blk, blk), 1)
    s = jnp.where(row >= colk, logits(qi) + d0_ref[...], NEG_INF)
    m = jnp.max(s, axis=-1, keepdims=True)
    p = jnp.exp(s - m)
    l = jnp.sum(p, axis=-1, keepdims=True)
    acc = jnp.dot(p.astype(BF16), values(qi), preferred_element_type=F32)

    def body(n, carry):
        m, l, acc = carry
        chosen = jnp.sum(jnp.where(col == n, sel, 0.0), axis=-1, keepdims=True) > 0.5
        bias = jnp.where(n == qi - 1, d1_ref[...], bfar_ref[0:1, :])
        s = jnp.where(chosen, logits(n) + bias, NEG_INF)
        m_new = jnp.maximum(m, jnp.max(s, axis=-1, keepdims=True))
        alpha = jnp.exp(m - m_new)
        p = jnp.exp(s - m_new)
        l = alpha * l + jnp.sum(p, axis=-1, keepdims=True)
        acc = alpha * acc + jnp.dot(p.astype(BF16), values(n), preferred_element_type=F32)
        return m_new, l, acc

    m, l, acc = lax.fori_loop(0, qi, body, (m, l, acc))
    o_ref[...] = (acc / l).astype(o_ref.dtype)


def _moba_prompt(qkv, qkv_bf, rel_bias, B, S):
    blk = MOBA_BLOCK
    nb = S // blk
    group = N_HEADS // MOBA_KV_HEADS
    qcols = D_ATTN // HEAD_DIM
    i = np.arange(blk)
    d0 = _bias_table(rel_bias, i[:, None] - i[None, :])
    d1 = _bias_table(rel_bias, blk + i[:, None] - i[None, :])
    bfar = jnp.broadcast_to(rel_bias[_np_bucket(np.array(2 * blk))][:, None, None], (N_HEADS, 8, blk))
    return pl.pallas_call(
        functools.partial(_moba_prompt_kernel, nb=nb, group=group),
        grid=(B, N_HEADS, nb),
        in_specs=[
            pl.BlockSpec((blk, HEAD_DIM), lambda b, h, i: (b * nb + i, h)),
            pl.BlockSpec((S, HEAD_DIM), lambda b, h, i: (b, qcols + h // group)),
            pl.BlockSpec((S, HEAD_DIM), lambda b, h, i: (b, qcols + MOBA_KV_HEADS + h // group)),
            pl.BlockSpec((S, HEAD_DIM), lambda b, h, i: (b, qcols + h // group)),
            pl.BlockSpec((None, blk, blk), lambda b, h, i: (h, 0, 0)),
            pl.BlockSpec((None, blk, blk), lambda b, h, i: (h, 0, 0)),
            pl.BlockSpec((None, 8, blk), lambda b, h, i: (h, 0, 0)),
        ],
        out_specs=pl.BlockSpec((blk, HEAD_DIM), lambda b, h, i: (b * nb + i, h)),
        out_shape=jax.ShapeDtypeStruct((B * S, D_ATTN), BF16),
        scratch_shapes=[pltpu.VMEM((nb, HEAD_DIM), BF16)],
        compiler_params=_compiler_params(3),
        name="moba_prompt",
    )(qkv_bf, qkv_bf, qkv_bf, qkv, d0, d1, bfar)


PAGES_PER_STEP = 8
NSA_HEADS_PER_STEP = 4


def _moba_sample_kernel(pt_ref, x_ref, bias_ref, b0_ref, *rest, group):
    pages = rest[:PAGES_PER_STEP]
    o_ref, g_scr, m_scr, l_scr, o_scr = rest[PAGES_PER_STEP:]
    j = pl.program_id(1)
    nsteps = pl.num_programs(1)
    pages_per_blk = MOBA_BLOCK // pages[0].shape[0]
    blks_per_step = PAGES_PER_STEP // pages_per_blk
    x = x_ref[...]
    q_bf = x[:N_HEADS].astype(BF16)
    qf = q_bf.astype(F32)
    lanes = (group, HEAD_DIM)
    for kvh in range(MOBA_KV_HEADS):
        rows = slice(kvh * group, (kvh + 1) * group)
        for bi in range(blks_per_step):
            n = j * blks_per_step + bi
            prs = pages[bi * pages_per_blk:(bi + 1) * pages_per_blk]
            k = jnp.concatenate([p[:, 0, kvh, :] for p in prs], axis=0)
            v = jnp.concatenate([p[:, 1, kvh, :] for p in prs], axis=0)
            kmean = jnp.mean(k, axis=0, keepdims=True).astype(BF16).astype(F32)
            gate = jnp.sum(qf[rows] * kmean, axis=-1, keepdims=True)
            s = (lax.dot_general(q_bf[rows], k.astype(BF16), _NT, preferred_element_type=F32) * ATTN_SCALE
                 + bias_ref[n, rows, :])
            m = jnp.max(s, axis=-1, keepdims=True)
            p = jnp.exp(s - m)
            l = jnp.sum(p, axis=-1, keepdims=True)
            o = jnp.dot(p.astype(BF16), v.astype(BF16), preferred_element_type=F32)
            g_scr[n, rows, :] = jnp.broadcast_to(gate, lanes)
            m_scr[n, rows, :] = jnp.broadcast_to(m, lanes)
            l_scr[n, rows, :] = jnp.broadcast_to(l, lanes)
            o_scr[n, rows, :] = o

    @pl.when(j == nsteps - 1)
    def _():
        g = g_scr[...]
        nidx = lax.broadcasted_iota(jnp.int32, g.shape, 0).astype(F32)
        sel = jnp.zeros(g.shape, F32)
        for _ in range(MOBA_TOPK):
            mx = jnp.max(g, axis=0, keepdims=True)
            idx = jnp.min(jnp.where(g == mx, nidx, float(g.shape[0])), axis=0, keepdims=True)
            pick = nidx == idx
            sel = jnp.where(pick, 1.0, sel)
            g = jnp.where(pick, -jnp.inf, g)
        chosen = sel > 0.5
        k_new = x[N_HEADS:N_HEADS + MOBA_KV_HEADS].astype(BF16).astype(F32)
        v_new = x[N_HEADS + MOBA_KV_HEADS:N_HEADS + 2 * MOBA_KV_HEADS].astype(BF16).astype(F32)
        k_rows = jnp.concatenate([jnp.broadcast_to(k_new[i:i + 1], lanes) for i in range(MOBA_KV_HEADS)], axis=0)
        v_rows = jnp.concatenate([jnp.broadcast_to(v_new[i:i + 1], lanes) for i in range(MOBA_KV_HEADS)], axis=0)
        s_new = jnp.sum(qf * k_rows, axis=-1, keepdims=True) * ATTN_SCALE + b0_ref[...]
        m_blk = m_scr[...]
        m_tot = jnp.maximum(jnp.max(jnp.where(chosen, m_blk, NEG_INF), axis=0), s_new)
        w = jnp.where(chosen, jnp.exp(m_blk - m_tot[None]), 0.0)
        e_new = jnp.exp(s_new - m_tot)
        l_tot = jnp.sum(w * l_scr[...], axis=0) + e_new
        o_ref[...] = (jnp.sum(w * o_scr[...], axis=0) + e_new * v_rows) / l_tot


def _moba_sample(x_s, cache_a_kv, page_table, rel_bias):
    DB, n_pages = page_table.shape
    n_pool, page = cache_a_kv.shape[:2]
    group = N_HEADS // MOBA_KV_HEADS
    nblk = n_pages * page // MOBA_BLOCK
    kpos = np.arange(nblk * MOBA_BLOCK).reshape(nblk, MOBA_BLOCK)
    bias = jnp.transpose(_bias_table(rel_bias, PAST_LEN - kpos), (1, 0, 2))
    b0 = jnp.broadcast_to(rel_bias[0][:, None], (N_HEADS, HEAD_DIM))
    nsteps = n_pages // PAGES_PER_STEP
    rows = x_s.shape[1]

    def page_spec(i):
        return pl.BlockSpec((None, page, 2, MOBA_KV_HEADS, HEAD_DIM),
                            lambda b, j, pt: (pt[b, j * PAGES_PER_STEP + i], 0, 0, 0, 0))

    grid_spec = pltpu.PrefetchScalarGridSpec(
        num_scalar_prefetch=1,
        grid=(DB, nsteps),
        in_specs=[
            pl.BlockSpec((None, rows, HEAD_DIM), lambda b, j, pt: (b, 0, 0)),
            pl.BlockSpec((nblk, N_HEADS, MOBA_BLOCK), lambda b, j, pt: (0, 0, 0)),
            pl.BlockSpec((N_HEADS, HEAD_DIM), lambda b, j, pt: (0, 0)),
        ] + [page_spec(i) for i in range(PAGES_PER_STEP)],
        out_specs=pl.BlockSpec((None, N_HEADS, HEAD_DIM), lambda b, j, pt: (b, 0, 0)),
        scratch_shapes=[pltpu.VMEM((nblk, N_HEADS, HEAD_DIM), F32) for _ in range(4)],
    )
    return pl.pallas_call(
        functools.partial(_moba_sample_kernel, group=group),
        grid_spec=grid_spec,
        out_shape=jax.ShapeDtypeStruct((DB, N_HEADS, HEAD_DIM), F32),
        compiler_params=_compiler_params(2),
        name="moba_sample",
    )(page_table, x_s, bias, b0, *([cache_a_kv] * PAGES_PER_STEP))


DIL_R = DIL_PATTERNS[0][0] // DIL_PATTERNS[0][1]
DIL_WIDTH = D_ATTN + 2 * DIL_KV_HEADS * HEAD_DIM


def _dilated_prompt_kernel(q_ref, k_ref, v_ref, dcur_ref, dprev_ref, o_ref, *, heads):
    i = pl.program_id(3)
    R = DIL_R
    row = lax.broadcasted_iota(jnp.int32, (R, R), 0)
    col = lax.broadcasted_iota(jnp.int32, (R, R), 1)
    has_prev = jnp.full((R, R), i, jnp.int32) > 0
    cur = pl.ds(pl.multiple_of(i * R, R), R)
    prev = pl.ds(pl.multiple_of(jnp.maximum(i - 1, 0) * R, R), R)
    k_cur, v_cur, k_prev, v_prev = k_ref[cur, :], v_ref[cur, :], k_ref[prev, :], v_ref[prev, :]
    for hh in range(heads):
        qh = q_ref[:, hh * HEAD_DIM:(hh + 1) * HEAD_DIM]
        s = lax.dot_general(qh, k_cur, _NT, preferred_element_type=F32) * ATTN_SCALE + dcur_ref[hh]
        s = jnp.where(row >= col, s, NEG_INF)
        m = jnp.max(s, axis=-1, keepdims=True)
        p = jnp.exp(s - m)
        l = jnp.sum(p, axis=-1, keepdims=True)
        acc = jnp.dot(p.astype(BF16), v_cur, preferred_element_type=F32)
        m, l, acc = _flash_update(qh, k_prev, v_prev, dprev_ref[hh], (col >= row) & has_prev, m, l, acc)
        base = hh * 2 * HEAD_DIM
        o_ref[:, base:base + HEAD_DIM] = acc / l
        o_ref[:, base + HEAD_DIM:base + 2 * HEAD_DIM] = jnp.broadcast_to(m + jnp.log(l), (R, HEAD_DIM))


def _dilated_prompt_group(qkv_bf, rel_bias, gi, B, S):
    win, dil = DIL_PATTERNS[gi]
    R = DIL_R
    G = N_HEADS // DIL_KV_HEADS
    n_rows, n_cols = qkv_bf.shape
    nt = S // dil // R
    a = np.arange(R)
    dcur = _bias_table(rel_bias, (a[:, None] - a[None, :]) * dil).reshape(DIL_KV_HEADS, G, R, R)
    dprev = _bias_table(rel_bias, (R + a[:, None] - a[None, :]) * dil).reshape(DIL_KV_HEADS, G, R, R)
    view = qkv_bf.reshape(n_rows // dil, dil * n_cols)
    qw = G * HEAD_DIM
    q_blk0 = gi * DIL_WIDTH // qw
    k_blk0 = (gi * DIL_WIDTH + D_ATTN) // HEAD_DIM
    out_w = G * 2 * HEAD_DIM
    return pl.pallas_call(
        functools.partial(_dilated_prompt_kernel, heads=G),
        grid=(B, dil, DIL_KV_HEADS, nt),
        in_specs=[
            pl.BlockSpec((R, qw), lambda b, r, k, i: (b * nt + i, r * (n_cols // qw) + q_blk0 + k)),
            pl.BlockSpec((S // dil, HEAD_DIM), lambda b, r, k, i: (b, r * (n_cols // HEAD_DIM) + k_blk0 + k)),
            pl.BlockSpec((S // dil, HEAD_DIM),
                         lambda b, r, k, i: (b, r * (n_cols // HEAD_DIM) + k_blk0 + DIL_KV_HEADS + k)),
            pl.BlockSpec((None, G, R, R), lambda b, r, k, i: (k, 0, 0, 0)),
            pl.BlockSpec((None, G, R, R), lambda b, r, k, i: (k, 0, 0, 0)),
        ],
        out_specs=pl.BlockSpec((R, out_w), lambda b, r, k, i: (b * nt + i, r * DIL_KV_HEADS + k)),
        out_shape=jax.ShapeDtypeStruct((B * S // dil, dil * N_HEADS * 2 * HEAD_DIM), F32),
        compiler_params=_compiler_params(4),
        name="dilated_prompt",
    )(view, view, view, dcur, dprev).reshape(B * S, N_HEADS * 2 * HEAD_DIM)


def _dilated_merge_kernel(a_ref, b_ref, c_ref, o_ref):
    for h in range(N_HEADS):
        oc = slice(h * 2 * HEAD_DIM, h * 2 * HEAD_DIM + HEAD_DIM)
        lc = slice(h * 2 * HEAD_DIM + HEAD_DIM, (h + 1) * 2 * HEAD_DIM)
        lses = [r[:, lc] for r in (a_ref, b_ref, c_ref)]
        m = jnp.maximum(jnp.maximum(lses[0], lses[1]), lses[2])
        es = [jnp.exp(x - m) for x in lses]
        tot = es[0] + es[1] + es[2]
        o = (es[0] / tot) * a_ref[:, oc] + (es[1] / tot) * b_ref[:, oc] + (es[2] / tot) * c_ref[:, oc]
        o_ref[:, h * HEAD_DIM:(h + 1) * HEAD_DIM] = o.astype(o_ref.dtype)


def _dilated_merge(parts, tm):
    n, w = parts[0].shape
    spec = pl.BlockSpec((tm, w), lambda i: (i, 0))
    return pl.pallas_call(
        _dilated_merge_kernel,
        grid=(n // tm,),
        in_specs=[spec, spec, spec],
        out_specs=pl.BlockSpec((tm, D_ATTN), lambda i: (i, 0)),
        out_shape=jax.ShapeDtypeStruct((n, D_ATTN), BF16),
        compiler_params=_compiler_params(1),
        name="dilated_merge",
    )(*parts)


def _dilated_sample_kernel(x_ref, bias_ref, b1_ref, b2_ref, b3_ref, o_ref):
    G = N_HEADS // DIL_KV_HEADS
    rows_per_group = N_HEADS + 2 * DIL_KV_HEADS
    x = x_ref[...]
    for k in range(DIL_KV_HEADS):
        outs, lses = [], []
        for gi, buf in enumerate((b1_ref, b2_ref, b3_ref)):
            base = gi * rows_per_group
            q = x[base + k * G:base + (k + 1) * G].astype(BF16)
            k_new = x[base + N_HEADS + k:base + N_HEADS + k + 1].astype(BF16).astype(F32)
            v_new = x[base + N_HEADS + DIL_KV_HEADS + k:base + N_HEADS + DIL_KV_HEADS + k + 1].astype(BF16).astype(F32)
            hs = slice(k * G, (k + 1) * G)
            s = (lax.dot_general(q, buf[:, 0, k, :].astype(BF16), _NT, preferred_element_type=F32) * ATTN_SCALE
                 + bias_ref[gi, hs, :DIL_R])
            s_new = (jnp.sum(q.astype(F32) * k_new, axis=-1, keepdims=True) * ATTN_SCALE
                     + bias_ref[gi, hs, DIL_R:DIL_R + 1])
            m = jnp.maximum(jnp.max(s, axis=-1, keepdims=True), s_new)
            p = jnp.exp(s - m)
            e_new = jnp.exp(s_new - m)
            l = jnp.sum(p, axis=-1, keepdims=True) + e_new
            outs.append((jnp.dot(p.astype(BF16), buf[:, 1, k, :].astype(BF16), preferred_element_type=F32)
                         + e_new * v_new) / l)
            lses.append(m + jnp.log(l))
        m = jnp.maximum(jnp.maximum(lses[0], lses[1]), lses[2])
        es = [jnp.exp(z - m) for z in lses]
        tot = es[0] + es[1] + es[2]
        o_ref[k * G:(k + 1) * G, :] = (es[0] / tot) * outs[0] + (es[1] / tot) * outs[1] + (es[2] / tot) * outs[2]


def _dilated_sample(x_s, bufs, rel_bias):
    DB, rows, _ = x_s.shape
    R = DIL_R
    m = np.arange(R + 1)
    steps = np.concatenate([R - np.arange(R), [0]])
    bias = jnp.stack([jnp.pad(_bias_table(rel_bias, steps * dil), ((0, 0), (0, 2 * LANE - R - 1)))
                      for _, dil in DIL_PATTERNS])
    views, specs = [], []
    for buf, (win, dil) in zip(bufs, DIL_PATTERNS):
        assert buf.shape[1] == win
        views.append(buf.reshape((DB, R, dil) + buf.shape[2:]))
        specs.append(pl.BlockSpec((None, R, None) + buf.shape[2:], lambda b: (b, 0, 0, 0, 0, 0)))
    return pl.pallas_call(
        _dilated_sample_kernel,
        grid=(DB,),
        in_specs=[pl.BlockSpec((None, rows, HEAD_DIM), lambda b: (b, 0, 0)),
                  pl.BlockSpec(bias.shape, lambda b: (0, 0, 0))] + specs,
        out_specs=pl.BlockSpec((None, N_HEADS, HEAD_DIM), lambda b: (b, 0, 0)),
        out_shape=jax.ShapeDtypeStruct((DB, N_HEADS, HEAD_DIM), F32),
        compiler_params=_compiler_params(1),
        name="dilated_sample",
    )(x_s, bias, *views)


def _bucket_thresholds():
    table = _np_bucket(np.arange(2 * REL_MAX_DIST))
    return [int(np.argmax(table >= b)) for b in range(REL_BUCKETS)]


def _bias_from_dist(dist, tab_ref, h):
    thr = _bucket_thresholds()
    bias = jnp.full(dist.shape, tab_ref[h, 0], F32)
    for b in range(1, REL_BUCKETS):
        bias = jnp.where(dist >= thr[b], tab_ref[h, b], bias)
    return bias


def _compress_kernel(a_ref, pe_ref, w1_ref, w2_ref, o_ref):
    half = CMP_STRIDE * HEAD_DIM
    a = a_ref[...]
    a_lo = (a + pe_ref[:, :half]).astype(BF16)
    a_hi = (a + pe_ref[:, half:]).astype(BF16)
    w1 = w1_ref[...].astype(BF16)
    p_lo = jnp.dot(a_lo, w1[:half], preferred_element_type=F32)
    p_hi = jnp.dot(a_hi, w1[half:], preferred_element_type=F32)
    nch = a.shape[0]
    hid = jax.nn.gelu(p_lo + pltpu.roll(p_hi, nch - 1, 0))
    o_ref[...] = jnp.dot(hid.astype(BF16), w2_ref[...].astype(BF16), preferred_element_type=F32).astype(o_ref.dtype)


def _compress_call(chunks, w1, w2, pe):
    G, _, nch, width = chunks.shape
    pe_flat = pe.reshape(2, 1, CMP_BLOCK * HEAD_DIM)
    return pl.pallas_call(
        _compress_kernel,
        grid=(2, G),
        in_specs=[
            pl.BlockSpec((None, None, nch, width), lambda c, g: (g, c, 0, 0)),
            pl.BlockSpec((None, 1, CMP_BLOCK * HEAD_DIM), lambda c, g: (c, 0, 0)),
            pl.BlockSpec((None, CMP_BLOCK * HEAD_DIM, w1.shape[-1]), lambda c, g: (c, 0, 0)),
            pl.BlockSpec((None, w2.shape[1], HEAD_DIM), lambda c, g: (c, 0, 0)),
        ],
        out_specs=pl.BlockSpec((None, None, nch, HEAD_DIM), lambda c, g: (g, c, 0, 0)),
        out_shape=jax.ShapeDtypeStruct((G, 2, nch, HEAD_DIM), BF16),
        compiler_params=_compiler_params(2),
        name="nsa_compress",
    )(chunks, pe_flat, w1, w2)


def _nsa_cmp_kernel(tab_ref, q_ref, gl_ref, kc_ref, vc_ref, ov_ref, o_ref, sel_ref, *, pos0, tq):
    qt = pl.program_id(1)
    ncmp = kc_ref.shape[0]
    nsel = ov_ref.shape[1]
    pos = pos0 + qt * tq + lax.broadcasted_iota(jnp.int32, (tq, 1), 0)
    ncol = lax.broadcasted_iota(jnp.int32, (tq, ncmp), 1)
    dc = pos - (CMP_STRIDE * ncol + CMP_BLOCK - 1)
    valid = dc >= 0
    kc = kc_ref[...]
    vc = vc_ref[...]
    gates = jax.nn.sigmoid(gl_ref[...])
    pcsum = jnp.zeros((tq, ncmp), F32)
    for h in range(N_HEADS):
        qh = q_ref[:, h * HEAD_DIM:(h + 1) * HEAD_DIM]
        lc = lax.dot_general(qh, kc, _NT, preferred_element_type=F32) * ATTN_SCALE + _bias_from_dist(dc, tab_ref, h)
        lc = jnp.where(valid, lc, NEG_INF)
        m = jnp.max(lc, axis=-1, keepdims=True)
        e = jnp.where(valid, jnp.exp(lc - m), 0.0)
        s = jnp.maximum(jnp.sum(e, axis=-1, keepdims=True), 1e-30)
        pc = e / s
        pcsum = pcsum + pc
        o_cmp = jnp.dot(pc.astype(BF16), vc, preferred_element_type=F32)
        o_ref[:, h * HEAD_DIM:(h + 1) * HEAD_DIM] = gates[:, 3 * h:3 * h + 1] * o_cmp
    imp = jnp.dot(pcsum.astype(BF16), ov_ref[...], preferred_element_type=F32)
    j = lax.broadcasted_iota(jnp.int32, (tq, nsel), 1)
    cur = jnp.right_shift(pos, int(math.log2(SEL_BLOCK)))
    forced = (j == 0) | (j == cur) | (j == cur - 1)
    score = jnp.where(j > cur, NEG_INF, jnp.where(forced, FORCE_SCORE, imp))
    rank = jnp.zeros((tq, nsel), F32)
    for jp in range(nsel):
        sj = score[:, jp:jp + 1]
        ahead = (sj > score) | ((sj == score) & (j > jp))
        rank = rank + jnp.where(ahead, 1.0, 0.0)
    sel_ref[...] = jnp.where(rank < float(min(SEL_TOPK, nsel)), 1.0, 0.0).astype(sel_ref.dtype)


def _flash_update(qh, k, v, bias, valid, m, l, acc):
    s = lax.dot_general(qh, k, _NT, preferred_element_type=F32) * ATTN_SCALE + bias
    s = jnp.where(valid, s, NEG_INF)
    m_new = jnp.maximum(m, jnp.max(s, axis=-1, keepdims=True))
    alpha = jnp.exp(m - m_new)
    p = jnp.exp(s - m_new)
    l = alpha * l + jnp.sum(p, axis=-1, keepdims=True)
    acc = alpha * acc + jnp.dot(p.astype(BF16), v, preferred_element_type=F32)
    return m_new, l, acc


def _nsa_selwin_kernel(q_ref, ks_ref, vs_ref, kw_ref, vw_ref, selm_ref, e_ref, gl_ref, ocmp_ref, d0_ref, d1_ref,
                       bfar_ref, o_ref, kmask_ref, *, tq, hp):
    qt = pl.program_id(1)
    hg = pl.program_id(2)

    @pl.when(hg == 0)
    def _():
        kmask_ref[...] = jnp.dot(selm_ref[...], e_ref[...], preferred_element_type=F32)

    qs = [q_ref[:, i * HEAD_DIM:(i + 1) * HEAD_DIM] for i in range(hp)]
    row = lax.broadcasted_iota(jnp.int32, (tq, tq), 0)
    colk = lax.broadcasted_iota(jnp.int32, (tq, tq), 1)
    causal = row >= colk

    def tile(ref, t):
        return ref[pl.ds(pl.multiple_of(t * tq, tq), tq), :]

    def kmask(t):
        return kmask_ref[:, pl.ds(pl.multiple_of(t * tq, tq), tq)] > 0.5

    def first(i, k, v, valid):
        s = lax.dot_general(qs[i], k, _NT, preferred_element_type=F32) * ATTN_SCALE + d0_ref[i]
        s = jnp.where(valid, s, NEG_INF)
        m = jnp.max(s, axis=-1, keepdims=True)
        p = jnp.exp(s - m)
        return m, jnp.sum(p, axis=-1, keepdims=True), jnp.dot(p.astype(BF16), v, preferred_element_type=F32)

    k_own, v_own, own_valid = tile(ks_ref, qt), tile(vs_ref, qt), causal & kmask(qt)
    states = tuple(first(i, k_own, v_own, own_valid) for i in range(hp))

    def sel_body(t, sts):
        k_t, v_t, chosen, is_prev = tile(ks_ref, t), tile(vs_ref, t), kmask(t), t == qt - 1
        return tuple(_flash_update(qs[i], k_t, v_t, jnp.where(is_prev, d1_ref[i], bfar_ref[i, 0:1, :]), chosen,
                                   *sts[i]) for i in range(hp))

    states = lax.fori_loop(0, qt, sel_body, states)
    o_sel = [acc / l for _, l, acc in states]

    k_own, v_own = tile(kw_ref, qt), tile(vw_ref, qt)
    states = [first(i, k_own, v_own, causal) for i in range(hp)]
    n_back = WIN // tq
    for delta in range(1, n_back + 1):
        t = jnp.maximum(qt - delta, 0)
        k_t, v_t = tile(kw_ref, t), tile(vw_ref, t)
        inside = jnp.full((tq, tq), qt, jnp.int32) >= delta
        valid = inside if delta < n_back else (row <= colk) & inside
        for i in range(hp):
            bias = d1_ref[i] if delta == 1 else bfar_ref[i, 0:1, :]
            states[i] = _flash_update(qs[i], k_t, v_t, bias, valid, *states[i])
    o_win = [acc / l for _, l, acc in states]

    gates = jax.nn.sigmoid(gl_ref[...])
    gcol = lax.broadcasted_iota(jnp.int32, gates.shape, 1)
    for i in range(hp):
        h = hg * hp + i
        g_sel = jnp.sum(jnp.where(gcol == 3 * h + 1, gates, 0.0), axis=-1, keepdims=True)
        g_win = jnp.sum(jnp.where(gcol == 3 * h + 2, gates, 0.0), axis=-1, keepdims=True)
        cols = slice(i * HEAD_DIM, (i + 1) * HEAD_DIM)
        o_ref[:, cols] = ((ocmp_ref[:, cols] + g_sel * o_sel[i]) + g_win * o_win[i]).astype(o_ref.dtype)


def _nsa_prompt(qkv, qkv_bf, gl, rel_bias, w_cmp1, w_cmp2, w_cmp_pe, B, S):
    tq = WIN_BLOCK
    nqt = S // tq
    nch = S // CMP_STRIDE
    nsel = S // SEL_BLOCK
    qcols = D_ATTN // HEAD_DIM
    kv_cmp = qkv[:B * S, D_ATTN:D_ATTN + 2 * HEAD_DIM].reshape(B, S, 2, HEAD_DIM)
    chunks = kv_cmp.transpose(0, 2, 1, 3).reshape(B, 2, nch, CMP_STRIDE * HEAD_DIM)
    cmp = _compress_call(chunks, w_cmp1, w_cmp2, w_cmp_pe)
    n = np.arange(nch)[:, None]
    jj = np.arange(nsel)[None, :]
    overlap = ((n * CMP_STRIDE < jj * SEL_BLOCK + SEL_BLOCK) & (n * CMP_STRIDE + CMP_BLOCK > jj * SEL_BLOCK)
               & (n < nch - 1))
    ov = jnp.asarray(overlap, BF16)
    expand = jnp.asarray(np.arange(S)[None, :] // SEL_BLOCK == np.arange(nsel)[:, None], BF16)
    tab = rel_bias.T
    ocmp, selm = pl.pallas_call(
        functools.partial(_nsa_cmp_kernel, pos0=0, tq=tq),
        grid=(B, nqt),
        in_specs=[
            pl.BlockSpec(memory_space=pltpu.SMEM),
            pl.BlockSpec((tq, D_ATTN), lambda b, t: (b * nqt + t, 0)),
            pl.BlockSpec((tq, 3 * N_HEADS), lambda b, t: (b * nqt + t, 0)),
            pl.BlockSpec((None, None, nch, HEAD_DIM), lambda b, t: (b, 0, 0, 0)),
            pl.BlockSpec((None, None, nch, HEAD_DIM), lambda b, t: (b, 1, 0, 0)),
            pl.BlockSpec((nch, nsel), lambda b, t: (0, 0)),
        ],
        out_specs=[
            pl.BlockSpec((tq, D_ATTN), lambda b, t: (b * nqt + t, 0)),
            pl.BlockSpec((tq, nsel), lambda b, t: (b * nqt + t, 0)),
        ],
        out_shape=[jax.ShapeDtypeStruct((B * S, D_ATTN), F32), jax.ShapeDtypeStruct((B * S, nsel), BF16)],
        compiler_params=_compiler_params(2),
        name="nsa_cmp",
    )(tab, qkv_bf, gl, cmp, cmp, ov)

    i = np.arange(tq)
    d0 = _bias_table(rel_bias, i[:, None] - i[None, :])
    d1 = _bias_table(rel_bias, tq + i[:, None] - i[None, :])
    bfar = jnp.broadcast_to(rel_bias[_np_bucket(np.array(2 * tq))][:, None, None], (N_HEADS, 8, tq))
    seq_spec = lambda c: pl.BlockSpec((S, HEAD_DIM), lambda b, t, h: (b, qcols + c))
    tile_spec = lambda w: pl.BlockSpec((tq, w), lambda b, t, h: (b * nqt + t, 0))
    hp = NSA_HEADS_PER_STEP
    head_tile = pl.BlockSpec((tq, hp * HEAD_DIM), lambda b, t, h: (b * nqt + t, h))
    return pl.pallas_call(
        functools.partial(_nsa_selwin_kernel, tq=tq, hp=hp),
        grid=(B, nqt, N_HEADS // hp),
        in_specs=[
            head_tile, seq_spec(2), seq_spec(3), seq_spec(4), seq_spec(5),
            tile_spec(nsel),
            pl.BlockSpec((nsel, S), lambda b, t, h: (0, 0)),
            tile_spec(3 * N_HEADS),
            head_tile,
            pl.BlockSpec((hp, tq, tq), lambda b, t, h: (h, 0, 0)),
            pl.BlockSpec((hp, tq, tq), lambda b, t, h: (h, 0, 0)),
            pl.BlockSpec((hp, 8, tq), lambda b, t, h: (h, 0, 0)),
        ],
        out_specs=head_tile,
        out_shape=jax.ShapeDtypeStruct((B * S, D_ATTN), BF16),
        scratch_shapes=[pltpu.VMEM((tq, S), F32)],
        compiler_params=_compiler_params(3),
        name="nsa_selwin",
    )(qkv_bf, qkv_bf, qkv_bf, qkv_bf, qkv_bf, selm, expand, gl, ocmp, d0, d1, bfar)


def _nsa_sample_cmp_kernel(pt_ref, x_ref, gl_ref, biasc_ref, ov_ref, w1_ref, w2_ref, pe_ref, *rest, n_cmp):
    pages = rest[:PAGES_PER_STEP]
    ocmp_ref, selm_ref, seq_scr = rest[PAGES_PER_STEP:]
    j = pl.program_id(1)
    nsteps = pl.num_programs(1)
    page = pages[0].shape[0]
    for i, p in enumerate(pages):
        start = pl.multiple_of((j * PAGES_PER_STEP + i) * page, page)
        for c in range(2):
            seq_scr[c, pl.ds(start, page), :] = p[:, c, :]

    @pl.when(j == nsteps - 1)
    def _():
        nch = seq_scr.shape[1] // CMP_STRIDE
        hidden = w1_ref.shape[-1]
        cmp = []
        for c in range(2):
            p_lo = jnp.zeros((nch, hidden), F32)
            p_hi = jnp.zeros((nch, hidden), F32)
            for r in range(CMP_STRIDE):
                rows = seq_scr[c, pl.ds(r, nch, stride=CMP_STRIDE), :]
                a_lo = (rows + pe_ref[c, r:r + 1, :]).astype(BF16)
                a_hi = (rows + pe_ref[c, CMP_STRIDE + r:CMP_STRIDE + r + 1, :]).astype(BF16)
                p_lo = p_lo + jnp.dot(a_lo, w1_ref[c, r * HEAD_DIM:(r + 1) * HEAD_DIM, :],
                                      preferred_element_type=F32)
                p_hi = p_hi + jnp.dot(a_hi, w1_ref[c, (CMP_STRIDE + r) * HEAD_DIM:(CMP_STRIDE + r + 1) * HEAD_DIM, :],
                                      preferred_element_type=F32)
            hid = jax.nn.gelu(p_lo + pltpu.roll(p_hi, nch - 1, 0))
            cmp.append(jnp.dot(hid.astype(BF16), w2_ref[c].astype(BF16), preferred_element_type=F32).astype(BF16))
        kc, vc = cmp
        x = x_ref[...]
        q = x[:N_HEADS].astype(BF16)
        ncol = lax.broadcasted_iota(jnp.int32, (N_HEADS, nch), 1)
        valid = ncol < n_cmp
        lc = lax.dot_general(q, kc, _NT, preferred_element_type=F32) * ATTN_SCALE + biasc_ref[...]
        lc = jnp.where(valid, lc, NEG_INF)
        m = jnp.max(lc, axis=-1, keepdims=True)
        e = jnp.where(valid, jnp.exp(lc - m), 0.0)
        pc = e / jnp.maximum(jnp.sum(e, axis=-1, keepdims=True), 1e-30)
        gates = jax.nn.sigmoid(gl_ref[...])
        ocmp_ref[...] = gates[:, 0:1] * jnp.dot(pc.astype(BF16), vc, preferred_element_type=F32)
        pcsum = jnp.broadcast_to(jnp.sum(pc, axis=0, keepdims=True), (8, nch))
        imp = jnp.dot(pcsum.astype(BF16), ov_ref[...], preferred_element_type=F32)
        jcol = lax.broadcasted_iota(jnp.int32, imp.shape, 1)
        cur = PAST_LEN // SEL_BLOCK
        forced = (jcol == 0) | (jcol == cur) | (jcol == cur - 1)
        score = jnp.where(jcol > cur, NEG_INF, jnp.where(forced, FORCE_SCORE, imp))
        rank = jnp.zeros(imp.shape, F32)
        for jp in range(cur + 1):
            sj = score[:, jp:jp + 1]
            ahead = (sj > score) | ((sj == score) & (jcol > jp))
            rank = rank + jnp.where(ahead, 1.0, 0.0)
        selm_ref[...] = jnp.where((rank < float(SEL_TOPK)) & (jcol <= cur), 1.0, 0.0)


def _nsa_sample_sel_kernel(info_ref, x_ref, gl_ref, ocmp_ref, tab_ref, biasw_ref, win_ref, *rest):
    blocks = rest[:SEL_TOPK]
    o_ref = rest[SEL_TOPK]
    b = pl.program_id(0)
    x = x_ref[...]
    q = x[:N_HEADS].astype(BF16)
    qf = q.astype(F32)
    tab = tab_ref[...]
    thr = _bucket_thresholds()
    n_cached = PAST_LEN // SEL_BLOCK

    def new_token(row_k, row_v):
        k_new = x[row_k:row_k + 1].astype(BF16).astype(F32)
        v_new = x[row_v:row_v + 1].astype(BF16).astype(F32)
        return jnp.sum(qf * k_new, axis=-1, keepdims=True) * ATTN_SCALE + tab[:, 0:1], v_new

    m, v_new = new_token(N_HEADS + 2, N_HEADS + 3)
    l = jnp.ones_like(m)
    acc = jnp.broadcast_to(v_new, (N_HEADS, HEAD_DIM))
    lane = lax.broadcasted_iota(jnp.int32, (1, SEL_BLOCK), 1)
    for k, blk in enumerate(blocks):
        sidx = info_ref[b, SEL_TOPK + k]
        dist = PAST_LEN - (sidx * SEL_BLOCK + lane)
        bias = jnp.broadcast_to(tab[:, 0:1], (N_HEADS, SEL_BLOCK))
        for bk in range(1, REL_BUCKETS):
            bias = jnp.where(dist >= thr[bk], tab[:, bk:bk + 1], bias)
        cached = jnp.full((N_HEADS, SEL_BLOCK), sidx, jnp.int32) < n_cached
        in_page = jnp.minimum(sidx, n_cached - 1) % (blk.shape[0] // SEL_BLOCK)
        rws = pl.ds(pl.multiple_of(in_page * SEL_BLOCK, SEL_BLOCK), SEL_BLOCK)
        m, l, acc = _flash_update(q, blk[rws, 2, :].astype(BF16), blk[rws, 3, :].astype(BF16), bias, cached,
                                  m, l, acc)
    o_sel = acc / l

    s_new, v_new = new_token(N_HEADS + 4, N_HEADS + 5)
    s = (lax.dot_general(q, win_ref[:, :HEAD_DIM].astype(BF16), _NT, preferred_element_type=F32) * ATTN_SCALE
         + biasw_ref[...])
    m = jnp.maximum(jnp.max(s, axis=-1, keepdims=True), s_new)
    p = jnp.exp(s - m)
    e_new = jnp.exp(s_new - m)
    l = jnp.sum(p, axis=-1, keepdims=True) + e_new
    o_win = (jnp.dot(p.astype(BF16), win_ref[:, HEAD_DIM:].astype(BF16), preferred_element_type=F32)
             + e_new * v_new) / l
    gates = jax.nn.sigmoid(gl_ref[...])
    o_ref[...] = (ocmp_ref[...] + gates[:, 1:2] * o_sel) + gates[:, 2:3] * o_win


def _nsa_sample(x_s, gl_s, cache_c_kv, state_c_win, page_table, rel_bias, w_cmp1, w_cmp2, w_cmp_pe):
    DB, n_pages = page_table.shape
    page = cache_c_kv.shape[1]
    L = n_pages * page
    assert L == PAST_LEN and state_c_win.shape[1] == WIN
    nch = L // CMP_STRIDE
    n_cmp = (L + 1 - CMP_BLOCK) // CMP_STRIDE + 1
    nsel_pad = 2 * LANE
    rows = x_s.shape[1]
    n = np.arange(nch)[:, None]
    jj = np.arange(nsel_pad)[None, :]
    overlap = ((n * CMP_STRIDE < jj * SEL_BLOCK + SEL_BLOCK) & (n * CMP_STRIDE + CMP_BLOCK > jj * SEL_BLOCK)
               & (n < n_cmp))
    ov = jnp.asarray(overlap, BF16)
    bias_c = _bias_table(rel_bias, PAST_LEN - (np.arange(nch) * CMP_STRIDE + CMP_BLOCK - 1))

    def page_spec(i):
        return pl.BlockSpec((None, page, 4, HEAD_DIM), lambda b, j, pt: (pt[b, j * PAGES_PER_STEP + i], 0, 0, 0))

    full = lambda shape: pl.BlockSpec(shape, lambda b, j, pt: (0,) * len(shape))
    ocmp, selm = pl.pallas_call(
        functools.partial(_nsa_sample_cmp_kernel, n_cmp=n_cmp),
        grid_spec=pltpu.PrefetchScalarGridSpec(
            num_scalar_prefetch=1,
            grid=(DB, n_pages // PAGES_PER_STEP),
            in_specs=[
                pl.BlockSpec((None, rows, HEAD_DIM), lambda b, j, pt: (b, 0, 0)),
                pl.BlockSpec((None, N_HEADS, 3), lambda b, j, pt: (b, 0, 0)),
                full((N_HEADS, nch)), full((nch, nsel_pad)), full(w_cmp1.shape), full(w_cmp2.shape),
                full(w_cmp_pe.shape),
            ] + [page_spec(i) for i in range(PAGES_PER_STEP)],
            out_specs=[pl.BlockSpec((None, N_HEADS, HEAD_DIM), lambda b, j, pt: (b, 0, 0)),
                       pl.BlockSpec((None, 8, nsel_pad), lambda b, j, pt: (b, 0, 0))],
            scratch_shapes=[pltpu.VMEM((2, L, HEAD_DIM), F32)],
        ),
        out_shape=[jax.ShapeDtypeStruct((DB, N_HEADS, HEAD_DIM), F32),
                   jax.ShapeDtypeStruct((DB, 8, nsel_pad), F32)],
        compiler_params=_compiler_params(2),
        name="nsa_sample_cmp",
    )(page_table, x_s, gl_s, bias_c, ov, w_cmp1.astype(BF16), w_cmp2, w_cmp_pe, *([cache_c_kv] * PAGES_PER_STEP))
    return _nsa_sample_sel(x_s, gl_s, ocmp, selm, cache_c_kv, state_c_win, page_table, rel_bias)


def _nsa_sample_sel(x_s, gl_s, ocmp, selm, cache_c_kv, state_c_win, page_table, rel_bias):
    DB, n_pages = page_table.shape
    page = cache_c_kv.shape[1]
    L = n_pages * page
    rows = x_s.shape[1]
    bias_w = _bias_table(rel_bias, WIN - np.arange(WIN))
    tab = rel_bias.T
    sidx = lax.top_k(selm[:, 0, :], SEL_TOPK)[1].astype(jnp.int32)
    n_cached = L // SEL_BLOCK
    halves = page // SEL_BLOCK

    pages = jnp.take_along_axis(page_table, jnp.minimum(sidx, n_cached - 1) // halves, axis=1)
    info = jnp.concatenate([pages, sidx], axis=1).astype(jnp.int32)

    def blk_spec(k):
        return pl.BlockSpec((None, page, 4, HEAD_DIM), lambda b, info: (info[b, k], 0, 0, 0))

    full2 = lambda shape: pl.BlockSpec(shape, lambda b, info: (0,) * len(shape))
    per_seq = lambda shape: pl.BlockSpec((None,) + shape, lambda b, info: (b,) + (0,) * len(shape))
    return pl.pallas_call(
        _nsa_sample_sel_kernel,
        grid_spec=pltpu.PrefetchScalarGridSpec(
            num_scalar_prefetch=1,
            grid=(DB,),
            in_specs=[per_seq((rows, HEAD_DIM)), per_seq((N_HEADS, 3)), per_seq((N_HEADS, HEAD_DIM)),
                      full2((N_HEADS, REL_BUCKETS)), full2((N_HEADS, WIN)), per_seq((WIN, 2 * HEAD_DIM))]
            + [blk_spec(k) for k in range(SEL_TOPK)],
            out_specs=per_seq((N_HEADS, HEAD_DIM)),
        ),
        out_shape=jax.ShapeDtypeStruct((DB, N_HEADS, HEAD_DIM), F32),
        compiler_params=_compiler_params(1),
        name="nsa_sample_sel",
    )(info, x_s, gl_s, ocmp, tab, bias_w, state_c_win.reshape(DB, WIN, 2 * HEAD_DIM), *([cache_c_kv] * SEL_TOPK))


def _mla_prompt_kernel(q_ref, lat_ref, wuv_ref, o_ref, acc_ref, *, tq):
    qi = pl.program_id(2)
    q = q_ref[...]

    def tile(t):
        return lat_ref[pl.ds(pl.multiple_of(t * tq, tq), tq), :]

    def scores(lat):
        return lax.dot_general(q, lat, _NT, preferred_element_type=F32) * MLA_SCALE

    lat = tile(qi)
    row = lax.broadcasted_iota(jnp.int32, (tq, tq), 0)
    col = lax.broadcasted_iota(jnp.int32, (tq, tq), 1)
    s = jnp.where(row >= col, scores(lat), NEG_INF)
    m = jnp.max(s, axis=-1, keepdims=True)
    p = jnp.exp(s - m)
    l = jnp.sum(p, axis=-1, keepdims=True)
    acc_ref[...] = jnp.dot(p.astype(BF16), lat[:, :MLA_KV_LORA], preferred_element_type=F32)

    def body(t, carry):
        m, l = carry
        lat = tile(t)
        s = scores(lat)
        m_new = jnp.maximum(m, jnp.max(s, axis=-1, keepdims=True))
        alpha = jnp.exp(m - m_new)
        p = jnp.exp(s - m_new)
        acc_ref[...] = alpha * acc_ref[...] + jnp.dot(p.astype(BF16), lat[:, :MLA_KV_LORA],
                                                      preferred_element_type=F32)
        return m_new, alpha * l + jnp.sum(p, axis=-1, keepdims=True)

    m, l = lax.fori_loop(0, qi, body, (m, l))
    o = (acc_ref[...] / l).astype(BF16)
    o_ref[...] = jnp.dot(o, wuv_ref[...].astype(BF16), preferred_element_type=F32).astype(o_ref.dtype)


def _mla_prompt(q_hm, lat_bf, w_kvb, B, S):
    tq = 512
    nq = S // tq
    width = MLA_KV_LORA + MLA_ROPE
    return pl.pallas_call(
        functools.partial(_mla_prompt_kernel, tq=tq),
        grid=(B, N_HEADS, nq),
        in_specs=[
            pl.BlockSpec((None, tq, width), lambda b, h, i: (h, b * nq + i, 0)),
            pl.BlockSpec((S, width), lambda b, h, i: (b, 0)),
            pl.BlockSpec((MLA_KV_LORA, MLA_V), lambda b, h, i: (0, 2 * h + 1)),
        ],
        out_specs=pl.BlockSpec((tq, MLA_V), lambda b, h, i: (b * nq + i, h)),
        out_shape=jax.ShapeDtypeStruct((B * S, N_HEADS * MLA_V), BF16),
        scratch_shapes=[pltpu.VMEM((tq, MLA_KV_LORA), F32)],
        compiler_params=_compiler_params(3),
        name="mla_prompt",
    )(q_hm, lat_bf, w_kvb)


def _mla_sample_kernel(pt_ref, q_ref, new_ref, *rest):
    pages = rest[:PAGES_PER_STEP]
    o_ref, m_scr, l_scr, acc_scr = rest[PAGES_PER_STEP:]
    j = pl.program_id(1)
    nsteps = pl.num_programs(1)
    q = q_ref[...].astype(BF16)
    lat_t = jnp.concatenate([p[...] for p in pages], axis=1).astype(BF16)
    s = jnp.dot(q, lat_t, preferred_element_type=F32) * MLA_SCALE
    m_blk = jnp.max(s, axis=-1, keepdims=True)

    @pl.when(j == 0)
    def _():
        m_scr[...] = jnp.full(m_scr.shape, NEG_INF, F32)
        l_scr[...] = jnp.zeros(l_scr.shape, F32)
        acc_scr[...] = jnp.zeros(acc_scr.shape, F32)

    m_old = m_scr[...]
    m_new = jnp.maximum(m_old, m_blk)
    alpha = jnp.exp(m_old - m_new)
    p = jnp.exp(s - m_new[:, 0:1])
    l_new = alpha * l_scr[...] + jnp.sum(p, axis=-1, keepdims=True)
    acc = alpha[:, 0:1] * acc_scr[...] + lax.dot_general(p.astype(BF16), lat_t[:MLA_KV_LORA], _NT,
                                                         preferred_element_type=F32)
    m_scr[...] = m_new
    l_scr[...] = l_new
    acc_scr[...] = acc

    @pl.when(j == nsteps - 1)
    def _():
        new = new_ref[...].astype(BF16).astype(F32)
        s_new = jnp.sum(q.astype(F32) * new, axis=-1, keepdims=True) * MLA_SCALE
        m_fin = jnp.maximum(m_new[:, 0:1], s_new)
        a = jnp.exp(m_new[:, 0:1] - m_fin)
        e = jnp.exp(s_new - m_fin)
        l_fin = a * l_new[:, 0:1] + e
        o_ref[...] = (a * acc + e * new[:, :MLA_KV_LORA]) / l_fin


def _mla_sample(q_s, lat_new, cache_d_latent, page_table):
    DB, n_pages = page_table.shape
    page, width = cache_d_latent.shape[1:]
    cache_t = jnp.swapaxes(cache_d_latent, 1, 2)

    def page_spec(i):
        return pl.BlockSpec((None, width, page), lambda b, j, pt: (pt[b, j * PAGES_PER_STEP + i], 0, 0))

    grid_spec = pltpu.PrefetchScalarGridSpec(
        num_scalar_prefetch=1,
        grid=(DB, n_pages // PAGES_PER_STEP),
        in_specs=[
            pl.BlockSpec((None, N_HEADS, width), lambda b, j, pt: (b, 0, 0)),
            pl.BlockSpec((None, 1, width), lambda b, j, pt: (b, 0, 0)),
        ] + [page_spec(i) for i in range(PAGES_PER_STEP)],
        out_specs=pl.BlockSpec((None, N_HEADS, MLA_KV_LORA), lambda b, j, pt: (b, 0, 0)),
        scratch_shapes=[pltpu.VMEM((N_HEADS, HEAD_DIM), F32), pltpu.VMEM((N_HEADS, HEAD_DIM), F32),
                        pltpu.VMEM((N_HEADS, MLA_KV_LORA), F32)],
    )
    return pl.pallas_call(
        _mla_sample_kernel,
        grid_spec=grid_spec,
        out_shape=jax.ShapeDtypeStruct((DB, N_HEADS, MLA_KV_LORA), F32),
        compiler_params=_compiler_params(2),
        name="mla_sample",
    )(page_table, q_s, lat_new, *([cache_t] * PAGES_PER_STEP))


def _rel_bucket(dist):
    n = jnp.maximum(dist, 0)
    max_exact = REL_BUCKETS // 2
    large = max_exact + (jnp.log(jnp.maximum(n, 1).astype(F32) / max_exact)
                         / math.log(REL_MAX_DIST / max_exact) * (REL_BUCKETS - max_exact)).astype(jnp.int32)
    large = jnp.minimum(large, REL_BUCKETS - 1)
    return jnp.where(n < max_exact, n, large)


def _softmax_lse(logits, mask, axis):
    l = jnp.where(mask, logits.astype(F32), NEG_INF)
    m = jnp.max(l, axis=axis, keepdims=True)
    e = jnp.where(mask, jnp.exp(l - m), 0.0)
    s = jnp.sum(e, axis=axis, keepdims=True)
    s = jnp.maximum(s, 1e-30)
    return e / s, jnp.squeeze(m + jnp.log(s), axis)


def _rope(x, pos):
    half = x.shape[-1] // 2
    inv = ROPE_THETA ** (-jnp.arange(half, dtype=F32) / half)
    ang = pos.astype(F32)[:, None] * inv[None, :]
    cos = jnp.cos(ang)[:, None, :]
    sin = jnp.sin(ang)[:, None, :]
    xf = x.astype(F32)
    x1, x2 = xf[..., :half], xf[..., half:]
    return jnp.concatenate([x1 * cos - x2 * sin, x1 * sin + x2 * cos], -1).astype(x.dtype)


def _rmsnorm_jax(x, g):
    xf = x.astype(F32)
    y = xf * lax.rsqrt(jnp.mean(xf * xf, axis=-1, keepdims=True) + RMS_EPS)
    return (y * g.astype(F32)).astype(x.dtype)


def _gather_pages(pool, pt_row):
    g = pool[pt_row]
    return g.reshape((-1,) + pool.shape[2:])


def _map_query_blocks(fn, q_arrays, q_pos, block):
    T = q_pos.shape[0]
    if T <= block or T % block:
        return fn(*q_arrays, q_pos)
    nb = T // block
    xs = tuple(a.reshape((nb, block) + a.shape[1:]) for a in q_arrays) + (q_pos.reshape(nb, block),)
    out = lax.map(lambda a: fn(*a), xs)
    return out.reshape((T,) + out.shape[2:])


def _moba_seq(q, kv, q_pos, rel_bias):
    L = kv.shape[0]
    nb = -(-L // MOBA_BLOCK)
    kv = jnp.pad(kv, ((0, nb * MOBA_BLOCK - L), (0, 0), (0, 0), (0, 0)))
    kvb = kv.reshape(nb, MOBA_BLOCK, 2, MOBA_KV_HEADS, HEAD_DIM).transpose(2, 3, 0, 1, 4)
    kb, vb = kvb[0], kvb[1]
    kmean = jnp.mean(kb.astype(F32), axis=2)
    topk = min(MOBA_TOPK, nb)
    G = N_HEADS // MOBA_KV_HEADS
    tb = rel_bias.T.reshape(MOBA_KV_HEADS, G, REL_BUCKETS)
    kv_i = jnp.arange(MOBA_KV_HEADS)[None, :, None, None]
    g_i = jnp.arange(G)[None, None, :, None, None]

    def block_fn(qb, pos):
        Tb = qb.shape[0]
        qg = qb.reshape(Tb, MOBA_KV_HEADS, G, HEAD_DIM)
        own = pos // MOBA_BLOCK
        gate = jnp.einsum('tkgd,knd->tkgn', qg.astype(F32), kmean)
        past = jnp.arange(nb)[None, :] < own[:, None]
        gate = jnp.where(past[:, None, None, :], gate, NEG_INF)
        _, sel = lax.top_k(gate, topk)
        sel_ok = sel < own[:, None, None, None]
        own_b = jnp.broadcast_to(own[:, None, None, None], (Tb, MOBA_KV_HEADS, G, 1))
        blocks = jnp.concatenate([sel, own_b], -1)
        ok_blk = jnp.concatenate([sel_ok, jnp.ones_like(own_b, dtype=bool)], -1)
        kg = kb[kv_i, blocks]
        vg = vb[kv_i, blocks]
        kpos = blocks[..., None] * MOBA_BLOCK + jnp.arange(MOBA_BLOCK)
        dist = pos[:, None, None, None, None] - kpos
        logits = (jnp.einsum('tkgd,tkgnsd->tkgns', qg, kg).astype(F32) * ATTN_SCALE
                  + tb[kv_i[..., None], g_i, _rel_bucket(dist)])
        mask = ok_blk[..., None] & (dist >= 0)
        nk = (topk + 1) * MOBA_BLOCK
        p, _ = _softmax_lse(logits.reshape(Tb, MOBA_KV_HEADS, G, nk), mask.reshape(Tb, MOBA_KV_HEADS, G, nk), -1)
        out = jnp.einsum('tkgj,tkgjd->tkgd', p.astype(vg.dtype), vg.reshape(Tb, MOBA_KV_HEADS, G, nk, HEAD_DIM))
        return out.reshape(Tb, N_HEADS, HEAD_DIM)

    return _map_query_blocks(block_fn, (q,), q_pos, GATHER_Q_BLOCK)


def _moba_core(qp, kvp, qs, kvs, cache_a_kv, page_table, pos_p, pos_s, rel_bias):
    op = lax.map(lambda a: _moba_seq(a[0], a[1], pos_p, rel_bias), (qp, kvp))

    def sample_seq(a):
        q, kv_new, pt = a
        kv_all = jnp.concatenate([_gather_pages(cache_a_kv, pt), kv_new], axis=0)
        return _moba_seq(q, kv_all, pos_s, rel_bias)
    os_ = lax.map(sample_seq, (qs, kvs, page_table))
    return op, os_


def _dilated_prompt(q, kv, win, dil, rel_bias):
    B, S = q.shape[:2]
    R = win // dil
    Sp = -(-S // win) * win
    nbu = Sp // win
    G = N_HEADS // DIL_KV_HEADS
    q = jnp.pad(q, ((0, 0), (0, Sp - S), (0, 0), (0, 0)))
    kv = jnp.pad(kv, ((0, 0), (0, Sp - S), (0, 0), (0, 0), (0, 0)))
    qr = q.reshape(B, nbu, R, dil, DIL_KV_HEADS, G, HEAD_DIM)
    kvr = kv.reshape(B, nbu, R, dil, 2, DIL_KV_HEADS, HEAD_DIM)
    prev = jnp.pad(kvr, ((0, 0), (1, 0), (0, 0), (0, 0), (0, 0), (0, 0), (0, 0)))[:, :nbu]
    kk = jnp.concatenate([prev, kvr], axis=2)
    a = jnp.arange(R)[:, None]
    j = jnp.arange(2 * R)[None, :]
    steps = R + a - j
    band = (steps >= 0) & (steps <= R)
    kvalid = (jnp.arange(nbu)[:, None] > 0) | (jnp.arange(2 * R)[None, :] >= R)
    ok = band[None] & kvalid[:, None, :]
    bias = rel_bias[_rel_bucket(steps * dil)].transpose(2, 0, 1).reshape(DIL_KV_HEADS, G, R, 2 * R)
    logits = jnp.einsum('bnqrkgd,bnjrkd->bnrkgqj', qr, kk[:, :, :, :, 0]).astype(F32) * ATTN_SCALE + bias
    p, lse = _softmax_lse(logits, ok[None, :, None, None, None, :, :], -1)
    out = jnp.einsum('bnrkgqj,bnjrkd->bnqrkgd', p.astype(kk.dtype), kk[:, :, :, :, 1])
    out = out.reshape(B, Sp, N_HEADS, HEAD_DIM)[:, :S]
    lse = lse.transpose(0, 1, 5, 2, 3, 4).reshape(B, Sp, N_HEADS)[:, :S]
    return out, lse


def _dilated_sample_jax(q, kv_new, buf, win, dil, rel_bias):
    DB, T = q.shape[:2]
    Lbuf = buf.shape[1]
    R = win // dil
    G = N_HEADS // DIL_KV_HEADS
    ext = jnp.concatenate([buf, kv_new], axis=1)
    m = jnp.arange(R + 1)
    idx = Lbuf + jnp.arange(T)[:, None] - m[None, :] * dil
    ok = idx >= 0
    g = ext[:, jnp.clip(idx, 0)]
    qg = q.reshape(DB, T, DIL_KV_HEADS, G, HEAD_DIM)
    bias = rel_bias[_rel_bucket(m * dil)].T.reshape(DIL_KV_HEADS, G, R + 1)
    logits = jnp.einsum('btkgd,btjkd->btkgj', qg, g[:, :, :, 0]).astype(F32) * ATTN_SCALE + bias
    p, lse = _softmax_lse(logits, ok[None, :, None, None, :], -1)
    out = jnp.einsum('btkgj,btjkd->btkgd', p.astype(g.dtype), g[:, :, :, 1])
    return out.reshape(DB, T, N_HEADS, HEAD_DIM), lse.reshape(DB, T, N_HEADS), ext[:, T:]


def _merge_groups(outs, lses):
    w = jax.nn.softmax(jnp.stack(lses, 0), axis=0)
    o = jnp.sum(w[..., None] * jnp.stack(outs, 0).astype(F32), axis=0)
    B, T = o.shape[:2]
    return o.reshape(B, T, D_ATTN).astype(outs[0].dtype)


def _dilated_core(qp, kvp, qs, kvs, bufs, rel_bias):
    S = qp.shape[1]
    outs_p, lses_p, outs_s, lses_s, new_p, new_s = [], [], [], [], [], []
    for gi, (win, dil) in enumerate(DIL_PATTERNS):
        o, l = _dilated_prompt(qp[:, :, gi], kvp[:, :, gi], win, dil, rel_bias)
        outs_p.append(o)
        lses_p.append(l)
        new_p.append(kvp[:, S - min(win, S):, gi])
        o, l, nbuf = _dilated_sample(qs[:, :, gi], kvs[:, :, gi], bufs[gi], win, dil, rel_bias)
        outs_s.append(o)
        lses_s.append(l)
        new_s.append(nbuf)
    return _merge_groups(outs_p, lses_p), _merge_groups(outs_s, lses_s), new_p, new_s


def _compress(kv2, w1, w2, pe):
    L = kv2.shape[0]
    n_cmp = (L - CMP_BLOCK) // CMP_STRIDE + 1
    idx = jnp.arange(n_cmp)[:, None] * CMP_STRIDE + jnp.arange(CMP_BLOCK)[None, :]
    blk = kv2[idx].transpose(2, 0, 1, 3) + pe[:, None]
    hid = jax.nn.gelu(jnp.einsum('cnx,cxh->cnh', blk.reshape(2, n_cmp, CMP_BLOCK * HEAD_DIM), w1))
    return jnp.einsum('cnh,chd->cnd', hid, w2)


def _nsa_seq(q, gates, kv4, q_pos, rel_bias, w_cmp1, w_cmp2, w_cmp_pe):
    L = kv4.shape[0]
    cmp = _compress(kv4[:, 0:2], w_cmp1, w_cmp2, w_cmp_pe)
    kc, vc = cmp[0], cmp[1]
    n_cmp = kc.shape[0]
    tok_start = jnp.arange(n_cmp) * CMP_STRIDE
    cmp_end = tok_start + CMP_BLOCK - 1
    n_sel = -(-L // SEL_BLOCK)
    sel_kv = jnp.pad(kv4[:, 2:4], ((0, n_sel * SEL_BLOCK - L), (0, 0), (0, 0))).reshape(n_sel, SEL_BLOCK, 2, HEAD_DIM)
    blk_start = jnp.arange(n_sel) * SEL_BLOCK
    overlap = ((tok_start[:, None] < blk_start[None, :] + SEL_BLOCK)
               & (tok_start[:, None] + CMP_BLOCK > blk_start[None, :])).astype(F32)
    n_top = min(SEL_TOPK, n_sel)

    def block_fn(qb, gb, pos):
        Tb = qb.shape[0]
        dc = pos[:, None] - cmp_end[None, :]
        lc = (jnp.einsum('thd,nd->thn', qb, kc).astype(F32) * ATTN_SCALE
              + rel_bias[_rel_bucket(dc)].transpose(0, 2, 1))
        pc, _ = _softmax_lse(lc, (dc >= 0)[:, None, :], -1)
        o_cmp = jnp.einsum('thn,nd->thd', pc.astype(vc.dtype), vc)
        imp = jnp.sum(pc, axis=1) @ overlap
        cur = pos // SEL_BLOCK
        j = jnp.arange(n_sel)[None, :]
        forced = (j == 0) | (j == cur[:, None]) | (j == cur[:, None] - 1)
        score = jnp.where(j > cur[:, None], NEG_INF, jnp.where(forced, FORCE_SCORE, imp))
        _, sidx = lax.top_k(score, n_top)
        g = sel_kv[sidx]
        kpos = sidx[..., None] * SEL_BLOCK + jnp.arange(SEL_BLOCK)
        ds = pos[:, None, None] - kpos
        ls = (jnp.einsum('thd,tnsd->thns', qb, g[..., 0, :]).astype(F32) * ATTN_SCALE
              + rel_bias[_rel_bucket(ds)].transpose(0, 3, 1, 2))
        nk = n_top * SEL_BLOCK
        ps, _ = _softmax_lse(ls.reshape(Tb, N_HEADS, nk), (ds >= 0).reshape(Tb, 1, nk), -1)
        o_sel = jnp.einsum('thj,tjd->thd', ps.astype(g.dtype), g[..., 1, :].reshape(Tb, nk, HEAD_DIM))
        return gb[..., 0:1] * o_cmp + gb[..., 1:2] * o_sel

    return _map_query_blocks(block_fn, (q, gates), q_pos, GATHER_Q_BLOCK)


def _window_prompt(q, kvw, rel_bias):
    B, S = q.shape[:2]
    P = WIN // WIN_BLOCK
    nb = S // WIN_BLOCK
    J = (P + 1) * WIN_BLOCK
    qr = q.reshape(B, nb, WIN_BLOCK, N_HEADS, HEAD_DIM)
    kvr = jnp.pad(kvw.reshape(B, nb, WIN_BLOCK, 2, HEAD_DIM), ((0, 0), (P, 0), (0, 0), (0, 0), (0, 0)))
    kk = jnp.concatenate([kvr[:, i:i + nb] for i in range(P + 1)], axis=2)
    dist = P * WIN_BLOCK + jnp.arange(WIN_BLOCK)[:, None] - jnp.arange(J)[None, :]
    kpos = jnp.arange(nb)[:, None] * WIN_BLOCK - P * WIN_BLOCK + jnp.arange(J)[None, :]
    ok = ((dist >= 0) & (dist <= WIN))[None] & (kpos >= 0)[:, None, :]
    bias = rel_bias[_rel_bucket(dist)].transpose(2, 0, 1)
    logits = jnp.einsum('bnqhd,bnjd->bnhqj', qr, kk[:, :, :, 0]).astype(F32) * ATTN_SCALE + bias
    p, _ = _softmax_lse(logits, ok[None, :, None], -1)
    out = jnp.einsum('bnhqj,bnjd->bnqhd', p.astype(kk.dtype), kk[:, :, :, 1])
    return out.reshape(B, S, N_HEADS, HEAD_DIM)


def _window_sample(q, kvw_new, buf, q_pos, rel_bias):
    T = q.shape[1]
    Lbuf = buf.shape[1]
    ext = jnp.concatenate([buf, kvw_new], axis=1)
    kpos = PAST_LEN - Lbuf + jnp.arange(Lbuf + T)
    dist = q_pos[:, None] - kpos[None, :]
    ok = (dist >= 0) & (dist <= WIN)
    bias = rel_bias[_rel_bucket(dist)].transpose(2, 0, 1)
    logits = jnp.einsum('bthd,bjd->bhtj', q, ext[:, :, 0]).astype(F32) * ATTN_SCALE + bias
    p, _ = _softmax_lse(logits, ok[None, None], -1)
    out = jnp.einsum('bhtj,bjd->bthd', p.astype(ext.dtype), ext[:, :, 1])
    return out, ext[:, T:]


def _nsa_sample_core(qs, kvs, gs, cache_c_kv, state_c_win, page_table, pos_s, rel_bias, w_cmp1, w_cmp2, w_cmp_pe):
    def sample_seq(a):
        q, g, kv_new, pt = a
        kv_all = jnp.concatenate([_gather_pages(cache_c_kv, pt), kv_new[:, :4]], axis=0)
        return _nsa_seq(q, g, kv_all, pos_s, rel_bias, w_cmp1, w_cmp2, w_cmp_pe)
    os_ = lax.map(sample_seq, (qs, gs, kvs, page_table))
    ow, win_s = _window_sample(qs, kvs[:, :, 4:6], state_c_win, pos_s, rel_bias)
    os_ = os_ + gs[..., 2:3] * ow
    return os_, win_s


def _mla_seq(q, lat, q_pos):
    kpos = jnp.arange(lat.shape[0])
    c = lat[:, :MLA_KV_LORA]

    def block_fn(qb, pos):
        logits = jnp.einsum('thc,lc->htl', qb, lat).astype(F32) * MLA_SCALE
        p, _ = _softmax_lse(logits, (kpos[None, :] <= pos[:, None])[None], -1)
        return jnp.einsum('htl,lc->thc', p.astype(c.dtype), c)

    return _map_query_blocks(block_fn, (q,), q_pos, DENSE_Q_BLOCK)


def kernel(x_prompt, x_sample, cache_a_kv, state_b_kv1, state_b_kv2, state_b_kv3, cache_c_kv, state_c_win,
           cache_d_latent, page_table, rel_bias, g_attn_norm, g_ffn_norm, g_final_norm,
           w_a_qkv, w_a_o, w_b_qkv, w_b_o, w_c_qkv, w_c_gate, w_c_cmp1, w_c_cmp2, w_c_cmp_pe, w_c_o,
           w_d_qa, g_d_qnorm, w_d_qb, w_d_kva, g_d_kvnorm, w_d_kvb, w_d_o, w_ffn_in, w_ffn_out):
    B, S, D = x_prompt.shape
    DB, T, _ = x_sample.shape
    NP = B * S
    NS = DB * T
    N = NP + NS
    TM = 640
    assert N % TM == 0
    pos_p = jnp.arange(S, dtype=jnp.int32)
    pos_s = PAST_LEN + jnp.arange(T, dtype=jnp.int32)

    x = jnp.concatenate([x_prompt.reshape(NP, D), x_sample.reshape(NS, D)], axis=0)
    w_ffn_out_bf = w_ffn_out.astype(BF16)

    def split(a):
        return a[:NP].reshape((B, S) + a.shape[1:]), a[NP:].reshape((DB, T) + a.shape[1:])

    outs = {}
    for layer in range(DEPTH):
        h = _rmsnorm(x, g_attn_norm[layer], BF16, TM)
        kind = layer % 4
        if kind == 0:
            qkv, qkv_bf = _matmul(h, w_a_qkv, tm=TM, tn=512, bf_copy=True)
            kv = qkv[:, D_ATTN:].reshape(N, 2, MOBA_KV_HEADS, HEAD_DIM)
            kvp, kvs = split(kv)
            op = _moba_prompt(qkv, qkv_bf, rel_bias, B, S)
            x_s = qkv[NP:].reshape(DB, N_HEADS + 2 * MOBA_KV_HEADS, HEAD_DIM)
            os_ = _moba_sample(x_s, cache_a_kv, page_table, rel_bias)
            o = jnp.concatenate([op, os_.reshape(NS, D_ATTN).astype(BF16)], axis=0)
            x = _matmul(o, w_a_o, tm=TM, tn=512, res=x)
            outs['a_kv'] = (kvp, kvs)
        elif kind == 1:
            n_g = len(DIL_PATTERNS)
            width = D_ATTN + 2 * DIL_KV_HEADS * HEAD_DIM
            qkv, qkv_bf = _matmul(h, w_b_qkv, tm=TM, tn=512, bf_copy=True)
            bufs = (state_b_kv1, state_b_kv2, state_b_kv3)
            kv = qkv.reshape(N, n_g, width)[..., D_ATTN:].reshape(N, n_g, 2, DIL_KV_HEADS, HEAD_DIM)
            kvp, kvs = split(kv)
            op = _dilated_merge([_dilated_prompt_group(qkv_bf, rel_bias, gi, B, S) for gi in range(n_g)], 256)
            x_s = qkv[NP:].reshape(DB, n_g * (N_HEADS + 2 * DIL_KV_HEADS), HEAD_DIM)
            os_ = _dilated_sample(x_s, bufs, rel_bias)
            new_p = [kvp[:, S - min(win, S):, gi] for gi, (win, _) in enumerate(DIL_PATTERNS)]
            new_s = [jnp.concatenate([buf, kvs[:, :, gi]], axis=1)[:, T:] for gi, buf in enumerate(bufs)]
            o = jnp.concatenate([op, os_.reshape(NS, D_ATTN).astype(BF16)], axis=0)
            x = _matmul(o, w_b_o, tm=TM, tn=512, res=x)
            outs['b'] = (new_p, new_s)
        elif kind == 2:
            qkv, qkv_bf = _matmul(h, w_c_qkv, tm=TM, tn=256, bf_copy=True)
            gl = _matmul(h, w_c_gate, tm=TM, tn=3 * N_HEADS)
            kv = qkv[:, D_ATTN:].reshape(N, NSA_KV_ROWS, HEAD_DIM)
            kvp, kvs = split(kv)
            op = _nsa_prompt(qkv, qkv_bf, gl, rel_bias, w_c_cmp1, w_c_cmp2, w_c_cmp_pe, B, S)
            x_s = qkv[NP:].reshape(DB, N_HEADS + NSA_KV_ROWS, HEAD_DIM)
            os_ = _nsa_sample(x_s, gl[NP:].reshape(DB, N_HEADS, 3), cache_c_kv, state_c_win, page_table, rel_bias,
                              w_c_cmp1, w_c_cmp2, w_c_cmp_pe)
            win_s = jnp.concatenate([state_c_win, kvs[:, :, 4:6]], axis=1)[:, T:]
            o = jnp.concatenate([op, os_.reshape(NS, D_ATTN).astype(BF16)], axis=0)
            x = _matmul(o, w_c_o, tm=TM, tn=512, res=x)
            outs['c'] = (kvp[:, :, :4], kvs[:, :, :4], kvp[:, S - min(WIN, S):, 4:6], win_s)
        else:
            w_kvb_r = w_d_kvb.reshape(MLA_KV_LORA, N_HEADS, MLA_NOPE + MLA_V)
            w_uk = w_kvb_r[..., :MLA_NOPE]
            w_uv = w_kvb_r[..., MLA_NOPE:]
            qa = _matmul(h, w_d_qa, tm=TM, tn=512)
            qn = _rmsnorm(qa, g_d_qnorm, BF16, TM)
            qfull = _matmul(qn, w_d_qb, tm=TM, tn=512).reshape(N, N_HEADS, MLA_NOPE + MLA_ROPE)
            kva = _matmul(h, w_d_kva, tm=TM, tn=MLA_KV_LORA + MLA_ROPE)
            pos_all = jnp.concatenate([jnp.tile(pos_p, B), jnp.tile(pos_s, DB)])

            def rope_rows(xr):
                half = xr.shape[-1] // 2
                inv = ROPE_THETA ** (-jnp.arange(half, dtype=F32) / half)
                ang = pos_all.astype(F32)[:, None] * inv[None, :]
                cos = jnp.cos(ang)[:, None, :]
                sin = jnp.sin(ang)[:, None, :]
                x1, x2 = xr[..., :half], xr[..., half:]
                return jnp.concatenate([x1 * cos - x2 * sin, x1 * sin + x2 * cos], -1)

            q_pe = rope_rows(qfull[..., MLA_NOPE:])
            q_lat = jnp.einsum('thn,chn->thc', qfull[..., :MLA_NOPE], w_uk)
            c = _rmsnorm_jax(kva[:, :MLA_KV_LORA], g_d_kvnorm)
            k_pe = rope_rows(kva[:, None, MLA_KV_LORA:])[:, 0, :]
            qcat = jnp.concatenate([q_lat, q_pe], -1)
            lat = jnp.concatenate([c, k_pe], -1)
            latp, lats = split(lat)
            q_hm = jnp.transpose(qcat[:NP], (1, 0, 2)).astype(BF16)
            op = _mla_prompt(q_hm, lat.astype(BF16), w_d_kvb, B, S)
            os_lat = _mla_sample(qcat[NP:], lat[NP:].reshape(DB, 1, MLA_KV_LORA + MLA_ROPE), cache_d_latent,
                                 page_table)
            os_ = jnp.einsum('thc,chv->thv', os_lat, w_uv).reshape(NS, N_HEADS * MLA_V)
            o = jnp.concatenate([op, os_.astype(BF16)], axis=0)
            x = _matmul(o, w_d_o, tm=TM, tn=512, res=x)
            outs['d'] = (latp, lats)

        h2 = _rmsnorm(x, g_ffn_norm[layer], BF16, TM)
        act = _ffn_in(h2, w_ffn_in, layer, tm=TM, tf=512)
        x = _matmul(act, w_ffn_out_bf, tm=TM, tn=512, res=x, w_layer=layer)

    y = _rmsnorm(x, g_final_norm, F32, TM)
    y_prompt = y[:NP].reshape(B, S, D)
    y_sample = y[NP:].reshape(DB, T, D)
    a_kv_prompt, a_kv_sample = outs['a_kv']
    new_p, new_s = outs['b']
    c_kv_prompt, c_kv_sample, c_win_prompt, c_win_sample = outs['c']
    d_latent_prompt, d_latent_sample = outs['d']
    return (y_prompt, y_sample, a_kv_prompt, a_kv_sample, new_p[0], new_s[0], new_p[1], new_s[1],
            new_p[2], new_s[2], c_kv_prompt, c_kv_sample, c_win_prompt, c_win_sample,
            d_latent_prompt, d_latent_sample)
```
